```python
import jax
import jax.numpy as jnp
from jax import lax
import numpy as np

D_MODEL = 2048
BATCH = 2
SEQ = 4096
DEPTH = 4
DEC_BATCH = 128
DEC_SEQ = 1
PAST_LEN = 8192
PAGE_SIZE = 128

N_BRANCH = 4
BR = D_MODEL // N_BRANCH
EPS = 1e-6
ROPE_THETA = 10000.0
Q_BLOCK = 128
A_HD = 64
A_HEADS = BR // A_HD
A_DECAY_LORA = 64
A_AAA_LORA = 64
A_IN = 3 * BR + A_DECAY_LORA + A_AAA_LORA
A_LN_EPS = 64e-5
DECAY_SCALE = 0.606531
CHUNK = 128
B_GROUPS = 4
B_GC = BR // B_GROUPS
B_IN = 2 * BR
C_HD = 64
C_HEADS = BR // C_HD
C_KV_HEADS = 2
C_IDX_HEADS = 4
C_IDX_HD = 64
C_TOPK = 256
C_IN = C_HEADS * C_HD + 2 * C_KV_HEADS * C_HD + C_IDX_HEADS * C_IDX_HD + C_IDX_HD + C_IDX_HEADS
D_NOPE = 64
D_ROPE = 32
D_V = 64
D_HEADS = BR // D_V
D_Q_LORA = 3 * D_MODEL // 16
D_KV_LORA = D_MODEL // 16
D_IN = D_Q_LORA + D_KV_LORA + D_ROPE
G_IN = N_BRANCH * D_MODEL
N_IN = A_IN + B_IN + C_IN + D_IN + G_IN
D_FF = ((8 * D_MODEL // 3 + 255) // 256) * 256
CONV_W = 3
PLE_DIM = 256

kernel_name = 'hybrid_rwkv7_gmlp_dsa_mla_decoder_step'


def rmsnorm(x, g, eps=EPS):
    xf = x.astype(jnp.float32)
    y = xf * lax.rsqrt(jnp.mean(xf * xf, axis=-1, keepdims=True) + eps)
    return (y * g.astype(jnp.float32)).astype(x.dtype)


def layernorm(x, g, b, eps):
    xf = x.astype(jnp.float32)
    mu = jnp.mean(xf, axis=-1, keepdims=True)
    var = jnp.mean(jnp.square(xf - mu), axis=-1, keepdims=True)
    y = (xf - mu) * lax.rsqrt(var + eps) * g.astype(jnp.float32) + b.astype(jnp.float32)
    return y.astype(x.dtype)


def rope(x, pos):
    half = x.shape[-1] // 2
    inv = ROPE_THETA ** (-jnp.arange(half, dtype=jnp.float32) / half)
    ang = pos.astype(jnp.float32)[:, None] * inv[None, :]
    shape = (1, x.shape[1]) + (1,) * (x.ndim - 3) + (half,)
    cos = jnp.cos(ang).reshape(shape)
    sin = jnp.sin(ang).reshape(shape)
    xf = x.astype(jnp.float32)
    x1, x2 = xf[..., :half], xf[..., half:]
    return jnp.concatenate([x1 * cos - x2 * sin, x2 * cos + x1 * sin], axis=-1).astype(x.dtype)


def batch_take(a, idx):
    return jax.vmap(lambda ab, ib: ab[ib])(a, idx)


def gather_pages(pool, page_table):
    rows = pool[page_table]
    return rows.reshape((page_table.shape[0], page_table.shape[1] * pool.shape[1]) + pool.shape[2:])


def paged_row_gather(pool, page_table, new_rows, idx, past_len):
    in_past = idx < past_len
    ip = jnp.minimum(idx, past_len - 1)
    phys = batch_take(page_table, ip // PAGE_SIZE)
    from_pool = pool[phys, ip % PAGE_SIZE]
    from_new = batch_take(new_rows, jnp.clip(idx - past_len, 0, new_rows.shape[1] - 1))
    mask = in_past.reshape(in_past.shape + (1,) * (from_pool.ndim - in_past.ndim))
    return jnp.where(mask, from_pool, from_new.astype(from_pool.dtype))


def map_query_blocks(fn, q_tensors, q_pos):
    T = q_pos.shape[0]
    if T <= Q_BLOCK:
        return fn(q_tensors, q_pos)
    nb = T // Q_BLOCK
    blocks = tuple(jnp.swapaxes(t.reshape((t.shape[0], nb, Q_BLOCK) + t.shape[2:]), 0, 1) for t in q_tensors)
    out = lax.map(lambda a: fn(a[0], a[1]), (blocks, q_pos.reshape(nb, Q_BLOCK)))
    out = jnp.swapaxes(out, 0, 1)
    return out.reshape((out.shape[0], T) + out.shape[3:])


def rwkv7_mixer(zA, shift_prev, S0, prm):
    Bn, T, _ = zA.shape
    f32 = jnp.float32
    z_prev = jnp.concatenate([shift_prev[:, None, :].astype(zA.dtype), zA[:, :-1]], axis=1)
    zs = zA + (z_prev - zA) * prm['a_mu']
    r, k, v = zs[..., :BR], zs[..., BR:2 * BR], zs[..., 2 * BR:3 * BR]
    xw, xa = zs[..., 3 * BR:3 * BR + A_DECAY_LORA], zs[..., 3 * BR + A_DECAY_LORA:]
    heads = lambda t: t.astype(f32).reshape(Bn, T, A_HEADS, A_HD)
    decay = jnp.exp(-DECAY_SCALE * jax.nn.sigmoid(heads(prm['a_w0'] + jnp.tanh(xw) @ prm['a_w2'])))
    a = jax.nn.sigmoid(heads(prm['a_a0'] + xa @ prm['a_a2']))
    r, k, v = heads(r), heads(k), heads(v)
    kk = k * prm['a_kk'].astype(f32).reshape(A_HEADS, A_HD)
    kk = kk / jnp.maximum(jnp.sqrt(jnp.sum(kk * kk, axis=-1, keepdims=True)), 1e-12)
    k = k * (1.0 + (a - 1.0) * prm['a_ka'].astype(f32).reshape(A_HEADS, A_HD))

    def step(S, inp):
        r_t, w_t, k_t, v_t, kk_t, a_t = inp
        sa = jnp.einsum('bhij,bhj->bhi', S, -kk_t)
        S = (S * w_t[:, :, None, :] + sa[..., None] * (kk_t * a_t)[:, :, None, :]
             + v_t[..., None] * k_t[:, :, None, :])
        return S, jnp.einsum('bhij,bhj->bhi', S, r_t)

    seq = tuple(jnp.moveaxis(t, 1, 0) for t in (r, decay, k, v, kk, a))
    S_T, ys = lax.scan(step, S0.astype(f32), seq)
    y = jnp.moveaxis(ys, 0, 1)
    mu = jnp.mean(y, axis=-1, keepdims=True)
    var = jnp.mean(jnp.square(y - mu), axis=-1, keepdims=True)
    y = ((y - mu) * lax.rsqrt(var + A_LN_EPS) * prm['a_lnx_g'].astype(f32).reshape(A_HEADS, A_HD)
         + prm['a_lnx_b'].astype(f32).reshape(A_HEADS, A_HD))
    bonus = jnp.sum(r * k * prm['a_rk'].astype(f32), axis=-1, keepdims=True) * v
    out = (y + bonus).reshape(Bn, T, BR).astype(zA.dtype)
    return out, S_T.astype(S0.dtype), zA[:, -1]


def chunk_mlp_mixer(zB, prm):
    Bn, T, _ = zB.shape
    u = zB[..., :BR]
    v = layernorm(zB[..., BR:], prm['b_ln_g'], prm['b_ln_b'], EPS)
    tc = min(T, CHUNK)
    vc = v.reshape(Bn, T // tc, tc, B_GROUPS, B_GC)
    causal = jnp.tril(jnp.ones((tc, tc), dtype=bool))
    ws = jnp.where(causal, prm['b_ws'][:, :tc, :tc], 0.0)
    s = jnp.einsum('gts,bnsgc->bntgc', ws, vc) + prm['b_bs'][:, :tc].T[None, None, :, :, None]
    return u * s.reshape(Bn, T, BR), v


def dsa_attention(q, qi, wi, ki_all, q_pos, gather_kv):
    L = ki_all.shape[1]
    n_sel = min(C_TOPK, L // 4)
    k_pos = jnp.arange(L)

    def block(qs, qp):
        qb, qib, wib = qs
        Bq, t = qb.shape[0], qb.shape[1]
        sc = jax.nn.relu(jnp.einsum('bthd,bsd->bths', qib, ki_all) * (C_IDX_HD ** -0.5))
        score = jnp.einsum('bth,bths->bts', wib, sc).astype(jnp.float32)
        score = jnp.where(k_pos[None, None, :] <= qp[None, :, None], score, -jnp.inf)
        top, idx = lax.top_k(score, n_sel)
        ks, vs = gather_kv(idx)
        qg = qb.reshape(Bq, t, C_KV_HEADS, C_HEADS // C_KV_HEADS, C_HD)
        s = jnp.einsum('btngd,btknd->btngk', qg, ks.astype(qg.dtype)).astype(jnp.float32) * (C_HD ** -0.5)
        s = jnp.where(jnp.isfinite(top)[:, :, None, None, :], s, -jnp.inf)
        pr = jax.nn.softmax(s, axis=-1).astype(qb.dtype)
        o = jnp.einsum('btngk,btknd->btngd', pr, vs.astype(qb.dtype))
        return o.reshape(Bq, t, C_HEADS * C_HD)

    return map_query_blocks(block, (q, qi, wi), q_pos)


def mla_attention(q_abs, q_pe, ckv_all, kpe_all, q_pos):
    k_pos = jnp.arange(ckv_all.shape[1])
    scale = (D_NOPE + D_ROPE) ** -0.5

    def block(qs, qp):
        qa, qr = qs
        s = (jnp.einsum('bthr,bsr->bhts', qa, ckv_all)
             + jnp.einsum('bthp,bsp->bhts', qr, kpe_all)).astype(jnp.float32) * scale
        s = jnp.where((k_pos[None, :] <= qp[:, None])[None, None], s, -jnp.inf)
        pr = jax.nn.softmax(s, axis=-1).astype(ckv_all.dtype)
        return jnp.einsum('bhts,bsr->bthr', pr, ckv_all)

    return map_query_blocks(block, (q_abs, q_pe), q_pos)


def conv_ffn(hn, buf, prm):
    T = hn.shape[1]
    up = hn @ prm['f_up']
    full = jnp.concatenate([buf.astype(up.dtype), up], axis=1)
    cw = prm['f_cw']
    c = prm['f_cb'] + full[:, 0:T] * cw[0]
    for j in range(1, CONV_W):
        c = c + full[:, j:j + T] * cw[j]
    y = (jax.nn.silu(c[..., :D_FF]) * c[..., D_FF:]) @ prm['f_down']
    return y, full[:, -(CONV_W - 1):]


def trunk_layer(h, pe, prm, pos, S0, shift0, conv0, paged):
    Bn, T, _ = h.shape
    z = rmsnorm(h, prm['norm_mix_g']) @ prm['w_in']
    o1 = A_IN
    o2 = o1 + B_IN
    o3 = o2 + C_IN
    o4 = o3 + D_IN
    zA, zB, zC, zD, zG = z[..., :o1], z[..., o1:o2], z[..., o2:o3], z[..., o3:o4], z[..., o4:]

    oA, S_new, shift_new = rwkv7_mixer(zA, shift0, S0, prm)

    oB, v_rows = chunk_mlp_mixer(zB, prm)

    c1 = C_HEADS * C_HD
    c2 = c1 + C_KV_HEADS * C_HD
    c3 = c2 + C_KV_HEADS * C_HD
    c4 = c3 + C_IDX_HEADS * C_IDX_HD
    c5 = c4 + C_IDX_HD
    q = rope(rmsnorm(zC[..., :c1].reshape(Bn, T, C_HEADS, C_HD), prm['c_qn_g']), pos)
    k = rope(rmsnorm(zC[..., c1:c2].reshape(Bn, T, C_KV_HEADS, C_HD), prm['c_kn_g']), pos)
    v = zC[..., c2:c3].reshape(Bn, T, C_KV_HEADS, C_HD)
    qi = rope(zC[..., c3:c4].reshape(Bn, T, C_IDX_HEADS, C_IDX_HD), pos)
    ki = rope(zC[..., c4:c5], pos)
    wi = zC[..., c5:] * (C_IDX_HEADS ** -0.5)
    if paged is None:
        ki_all = ki

        def gather_kv(idx):
            return batch_take(k, idx), batch_take(v, idx)
    else:
        pt = paged['page_table']
        past_len = pt.shape[1] * PAGE_SIZE
        ki_all = jnp.concatenate([gather_pages(paged['dsa_kidx'], pt).astype(ki.dtype), ki], axis=1)

        def gather_kv(idx):
            return (paged_row_gather(paged['dsa_k'], pt, k, idx, past_len),
                    paged_row_gather(paged['dsa_v'], pt, v, idx, past_len))
    oC = dsa_attention(q, qi, wi, ki_all, pos, gather_kv)

    d1 = D_Q_LORA
    d2 = d1 + D_KV_LORA
    cq = rmsnorm(zD[..., :d1], prm['d_qa_norm_g'])
    qd = rmsnorm((cq @ prm['d_w_uq']).reshape(Bn, T, D_HEADS, D_NOPE + D_ROPE), prm['d_qn_g'])
    q_nope = qd[..., :D_NOPE]
    q_pe = rope(qd[..., D_NOPE:], pos)
    ckv = rmsnorm(zD[..., d1:d2], prm['d_kva_norm_g'])
    kpe = rope(rmsnorm(zD[..., d2:], prm['d_kn_g']), pos)
    q_abs = jnp.einsum('bthn,rhn->bthr', q_nope, prm['d_w_uk'])
    if paged is None:
        ckv_all, kpe_all = ckv, kpe
    else:
        ckv_all = jnp.concatenate([gather_pages(paged['mla_ckv'], pt).astype(ckv.dtype), ckv], axis=1)
        kpe_all = jnp.concatenate([gather_pages(paged['mla_kpe'], pt).astype(kpe.dtype), kpe], axis=1)
    o_lat = mla_attention(q_abs, q_pe, ckv_all, kpe_all, pos)
    oD = jnp.einsum('bthr,rhd->bthd', o_lat, prm['d_w_uv']).reshape(Bn, T, BR)

    gates = jax.nn.sigmoid(zG.reshape(Bn, T, N_BRANCH, D_MODEL))
    branches = jnp.stack([oA, oB, oC, oD], axis=2)
    up = jnp.einsum('btkc,kcd->btkd', branches, prm['w_branch'])
    h = h + jnp.sum(gates * up, axis=2) @ prm['w_out']

    f, conv_new = conv_ffn(rmsnorm(h, prm['norm_ffn_g']), conv0, prm)
    h = h + f

    g_ple = jax.nn.sigmoid(rmsnorm(h, prm['norm_ple_g']) @ prm['w_ple_gate'])
    h = h + g_ple * (pe @ prm['w_ple_proj'])
    return h, (S_new, shift_new, conv_new, k, v, ki, ckv, kpe, v_rows)


def setup_inputs(seed: int = 0) -> dict:
    key = jax.random.key(seed)
    keys = iter(jax.random.split(key, 64))
    n_pages = PAST_LEN // PAGE_SIZE
    n_pool = (DEC_BATCH * n_pages * 5) // 4

    def nrm(shape, scale):
        return jax.random.normal(next(keys), shape, jnp.float32) * scale

    def gain(shape):
        return 1.0 + nrm(shape, 0.05)

    perm = jax.random.permutation(next(keys), n_pool)
    page_table = perm[:DEC_BATCH * n_pages].reshape(DEC_BATCH, n_pages).astype(jnp.int32)
    return {
        'x_prompt': nrm((BATCH, SEQ, D_MODEL), 1.0),
        'x_sample': nrm((DEC_BATCH, DEC_SEQ, D_MODEL), 1.0),
        'state_rwkv': nrm((DEPTH, DEC_BATCH, A_HEADS, A_HD, A_HD), 0.5),
        'state_rwkv_shift': nrm((DEPTH, DEC_BATCH, A_IN), 1.0),
        'state_ffn_conv': nrm((DEPTH, DEC_BATCH, CONV_W - 1, 2 * D_FF), 1.0),
        'cache_dsa_k': nrm((DEPTH, n_pool, PAGE_SIZE, C_KV_HEADS, C_HD), 1.0),
        'cache_dsa_v': nrm((DEPTH, n_pool, PAGE_SIZE, C_KV_HEADS, C_HD), 1.0),
        'cache_dsa_kidx': nrm((DEPTH, n_pool, PAGE_SIZE, C_IDX_HD), 1.0),
        'cache_mla_ckv': nrm((DEPTH, n_pool, PAGE_SIZE, D_KV_LORA), 1.0),
        'cache_mla_kpe': nrm((DEPTH, n_pool, PAGE_SIZE, D_ROPE), 1.0),
        'page_table': page_table,
        'p_prompt': nrm((DEPTH, BATCH, SEQ, PLE_DIM), 1.0),
        'p_sample': nrm((DEPTH, DEC_BATCH, DEC_SEQ, PLE_DIM), 1.0),
        'norm_mix_g': gain((DEPTH, D_MODEL)),
        'w_in': nrm((DEPTH, D_MODEL, N_IN), D_MODEL ** -0.5),
        'a_mu': jax.random.uniform(next(keys), (DEPTH, A_IN), jnp.float32),
        'a_w0': nrm((DEPTH, BR), 0.5),
        'a_w2': nrm((DEPTH, A_DECAY_LORA, BR), 0.1 * A_DECAY_LORA ** -0.5),
        'a_a0': nrm((DEPTH, BR), 0.1),
        'a_a2': nrm((DEPTH, A_AAA_LORA, BR), 0.1 * A_AAA_LORA ** -0.5),
        'a_kk': 0.85 + nrm((DEPTH, BR), 0.05),
        'a_ka': 1.0 + nrm((DEPTH, BR), 0.05),
        'a_rk': nrm((DEPTH, A_HEADS, A_HD), 0.1),
        'a_lnx_g': gain((DEPTH, BR)),
        'a_lnx_b': nrm((DEPTH, BR), 0.02),
        'b_ln_g': gain((DEPTH, BR)),
        'b_ln_b': nrm((DEPTH, BR), 0.02),
        'b_ws': nrm((DEPTH, B_GROUPS, CHUNK, CHUNK), CHUNK ** -0.5),
        'b_bs': 1.0 + nrm((DEPTH, B_GROUPS, CHUNK), 0.01),
        'c_qn_g': gain((DEPTH, C_HD)),
        'c_kn_g': gain((DEPTH, C_HD)),
        'd_qa_norm_g': gain((DEPTH, D_Q_LORA)),
        'd_w_uq': nrm((DEPTH, D_Q_LORA, D_HEADS * (D_NOPE + D_ROPE)), D_Q_LORA ** -0.5),
        'd_qn_g': gain((DEPTH, D_NOPE + D_ROPE)),
        'd_kva_norm_g': gain((DEPTH, D_KV_LORA)),
        'd_kn_g': gain((DEPTH, D_ROPE)),
        'd_w_uk': nrm((DEPTH, D_KV_LORA, D_HEADS, D_NOPE), D_KV_LORA ** -0.5),
        'd_w_uv': nrm((DEPTH, D_KV_LORA, D_HEADS, D_V), D_KV_LORA ** -0.5),
        'w_branch': nrm((DEPTH, N_BRANCH, BR, D_MODEL), BR ** -0.5),
        'w_out': nrm((DEPTH, D_MODEL, D_MODEL), 0.5 * D_MODEL ** -0.5),
        'norm_ffn_g': gain((DEPTH, D_MODEL)),
        'f_up': nrm((DEPTH, D_MODEL, 2 * D_FF), D_MODEL ** -0.5),
        'f_cw': nrm((DEPTH, CONV_W, 2 * D_FF), CONV_W ** -0.5),
        'f_cb': nrm((DEPTH, 2 * D_FF), 0.02),
        'f_down': nrm((DEPTH, D_FF, D_MODEL), D_FF ** -0.5),
        'norm_ple_g': gain((DEPTH, D_MODEL)),
        'w_ple_gate': nrm((DEPTH, D_MODEL, D_MODEL), D_MODEL ** -0.5),
        'w_ple_proj': nrm((DEPTH, PLE_DIM, D_MODEL), PLE_DIM ** -0.5),
    }


def reference(x_prompt, x_sample, state_rwkv, state_rwkv_shift, state_ffn_conv,
              cache_dsa_k, cache_dsa_v, cache_dsa_kidx, cache_mla_ckv, cache_mla_kpe,
              page_table, p_prompt, p_sample,
              norm_mix_g, w_in, a_mu, a_w0, a_w2, a_a0, a_a2, a_kk, a_ka, a_rk, a_lnx_g, a_lnx_b,
              b_ln_g, b_ln_b, b_ws, b_bs, c_qn_g, c_kn_g,
              d_qa_norm_g, d_w_uq, d_qn_g, d_kva_norm_g, d_kn_g, d_w_uk, d_w_uv,
              w_branch, w_out, norm_ffn_g, f_up, f_cw, f_cb, f_down,
              norm_ple_g, w_ple_gate, w_ple_proj):
    params = dict(norm_mix_g=norm_mix_g, w_in=w_in, a_mu=a_mu, a_w0=a_w0, a_w2=a_w2, a_a0=a_a0,
                  a_a2=a_a2, a_kk=a_kk, a_ka=a_ka, a_rk=a_rk, a_lnx_g=a_lnx_g, a_lnx_b=a_lnx_b,
                  b_ln_g=b_ln_g, b_ln_b=b_ln_b, b_ws=b_ws, b_bs=b_bs, c_qn_g=c_qn_g, c_kn_g=c_kn_g,
                  d_qa_norm_g=d_qa_norm_g, d_w_uq=d_w_uq, d_qn_g=d_qn_g, d_kva_norm_g=d_kva_norm_g,
                  d_kn_g=d_kn_g, d_w_uk=d_w_uk, d_w_uv=d_w_uv, w_branch=w_branch, w_out=w_out,
                  norm_ffn_g=norm_ffn_g, f_up=f_up, f_cw=f_cw, f_cb=f_cb, f_down=f_down,
                  norm_ple_g=norm_ple_g, w_ple_gate=w_ple_gate, w_ple_proj=w_ple_proj)
    past_len = page_table.shape[1] * PAGE_SIZE
    Bp, Tp = x_prompt.shape[0], x_prompt.shape[1]
    pos_p = jnp.arange(Tp, dtype=jnp.int32)
    pos_s = past_len + jnp.arange(x_sample.shape[1], dtype=jnp.int32)
    dt = x_prompt.dtype
    hp, hs = x_prompt, x_sample
    sp, ss = [], []
    for i in range(DEPTH):
        prm = {name: arr[i] for name, arr in params.items()}
        S0 = jnp.zeros((Bp, A_HEADS, A_HD, A_HD), dt)
        sh0 = jnp.zeros((Bp, A_IN), dt)
        cv0 = jnp.zeros((Bp, CONV_W - 1, 2 * D_FF), dt)
        hp, st_p = trunk_layer(hp, p_prompt[i], prm, pos_p, S0, sh0, cv0, None)
        paged = dict(page_table=page_table, dsa_k=cache_dsa_k[i], dsa_v=cache_dsa_v[i],
                     dsa_kidx=cache_dsa_kidx[i], mla_ckv=cache_mla_ckv[i], mla_kpe=cache_mla_kpe[i])
        hs, st_s = trunk_layer(hs, p_sample[i], prm, pos_s, state_rwkv[i], state_rwkv_shift[i],
                               state_ffn_conv[i], paged)
        sp.append(st_p)
        ss.append(st_s)

    def stk(lst, j):
        return jnp.stack([s[j] for s in lst], axis=0)

    return (hp, hs,
            stk(sp, 0), stk(sp, 1), stk(sp, 2), stk(sp, 3), stk(sp, 4), stk(sp, 5), stk(sp, 6), stk(sp, 7),
            stk(ss, 0), stk(ss, 1), stk(ss, 2), stk(ss, 3), stk(ss, 4), stk(ss, 5), stk(ss, 6), stk(ss, 7),
            stk(ss, 8))
```

```python
import functools

import jax
import jax.numpy as jnp
from jax import lax
from jax.experimental import pallas as pl
from jax.experimental.pallas import tpu as pltpu

F32 = jnp.float32
BF16 = jnp.bfloat16
I32 = jnp.int32

D_MODEL = 2048
N_BRANCH = 4
BR = D_MODEL // N_BRANCH
EPS = 1e-6
ROPE_THETA = 10000.0
Q_BLOCK = 128
A_HD = 64
A_HEADS = BR // A_HD
A_DECAY_LORA = 64
A_AAA_LORA = 64
A_IN = 3 * BR + A_DECAY_LORA + A_AAA_LORA
A_LN_EPS = 64e-5
DECAY_SCALE = 0.606531
CHUNK = 128
B_GROUPS = 4
B_GC = BR // B_GROUPS
B_IN = 2 * BR
C_HD = 64
C_HEADS = BR // C_HD
C_KV_HEADS = 2
C_IDX_HEADS = 4
C_IDX_HD = 64
C_TOPK = 256
C_IN = C_HEADS * C_HD + 2 * C_KV_HEADS * C_HD + C_IDX_HEADS * C_IDX_HD + C_IDX_HD + C_IDX_HEADS
D_NOPE = 64
D_ROPE = 32
D_V = 64
D_HEADS = BR // D_V
D_Q_LORA = 3 * D_MODEL // 16
D_KV_LORA = D_MODEL // 16
D_IN = D_Q_LORA + D_KV_LORA + D_ROPE
PAGE_SIZE = 128

LANE = 128
C_PAD = 1152
D_PAD = 640
N_MIX = A_IN + B_IN + C_PAD + D_PAD
VMEM_LIMIT = 56 * 1024 * 1024
NEG = -1e30
INT_MIN = -2147483648


def _pick_tile(n, target, mult=8):
    best = n
    for t in range(mult, min(n, target) + 1, mult):
        if n % t == 0:
            best = t
    return best


def _cparams(sem):
    return pltpu.CompilerParams(dimension_semantics=sem, vmem_limit_bytes=VMEM_LIMIT)


def _dot(a, b):
    return jnp.dot(a, b, preferred_element_type=F32)


def _dot_nt(a, b):
    return lax.dot_general(a, b, (((1,), (1,)), ((), ())), preferred_element_type=F32)


def _norm_mm_kernel(h_ref, g_ref, w_ref, *rest, emit_xn):
    if emit_xn:
        xn_ref, z_ref, xs_ref = rest
    else:
        z_ref, xs_ref = rest

    @pl.when(pl.program_id(1) == 0)
    def _():
        x = h_ref[...]
        ms = jnp.mean(x * x, axis=-1, keepdims=True)
        xn = (x * lax.rsqrt(ms + EPS) * g_ref[...]).astype(BF16)
        xs_ref[...] = xn
        if emit_xn:
            xn_ref[...] = xn

    z_ref[...] = _dot(xs_ref[...], w_ref[...])


def norm_matmul(h, g, w, *, tn, emit_xn, tm_target=640):
    m, k = h.shape
    n = w.shape[1]
    tm = _pick_tile(m, tm_target)
    out_shape = [jax.ShapeDtypeStruct((m, n), F32)]
    out_specs = [pl.BlockSpec((tm, tn), lambda i, j: (i, j))]
    if emit_xn:
        out_shape.insert(0, jax.ShapeDtypeStruct((m, k), BF16))
        out_specs.insert(0, pl.BlockSpec((tm, k), lambda i, j: (i, 0)))
    outs = pl.pallas_call(
        functools.partial(_norm_mm_kernel, emit_xn=emit_xn),
        grid=(m // tm, n // tn),
        in_specs=[pl.BlockSpec((tm, k), lambda i, j: (i, 0)),
                  pl.BlockSpec((1, k), lambda i, j: (0, 0)),
                  pl.BlockSpec((k, tn), lambda i, j: (0, j))],
        out_specs=out_specs,
        out_shape=out_shape,
        scratch_shapes=[pltpu.VMEM((tm, k), BF16)],
        compiler_params=_cparams(("parallel", "arbitrary")),
        name="norm_matmul",
    )(h, g.reshape(1, k), w)
    return outs if emit_xn else outs[0]


def _gate_merge_kernel(xn_ref, br_ref, wg_ref, wb_ref, o_ref, acc_ref):
    kk = pl.program_id(2)
    gate = jax.nn.sigmoid(_dot(xn_ref[...], wg_ref[...]))
    up = _dot(br_ref[0], wb_ref[0])
    contrib = gate * up

    @pl.when(kk == 0)
    def _():
        acc_ref[...] = contrib

    @pl.when(kk > 0)
    def _():
        acc_ref[...] += contrib

    @pl.when(kk == N_BRANCH - 1)
    def _():
        o_ref[...] = acc_ref[...].astype(BF16)


def gate_merge(xn, br, wg, wb, *, tn=512, tm_target=640):
    m, k = xn.shape
    tm = _pick_tile(m, tm_target)
    nj = D_MODEL // tn
    return pl.pallas_call(
        _gate_merge_kernel,
        grid=(m // tm, nj, N_BRANCH),
        in_specs=[pl.BlockSpec((tm, k), lambda i, j, c: (i, 0)),
                  pl.BlockSpec((1, tm, BR), lambda i, j, c: (c, i, 0)),
                  pl.BlockSpec((k, tn), lambda i, j, c: (0, c * nj + j)),
                  pl.BlockSpec((1, BR, tn), lambda i, j, c: (c, 0, j))],
        out_specs=pl.BlockSpec((tm, tn), lambda i, j, c: (i, j)),
        out_shape=jax.ShapeDtypeStruct((m, D_MODEL), BF16),
        scratch_shapes=[pltpu.VMEM((tm, tn), F32)],
        compiler_params=_cparams(("parallel", "parallel", "arbitrary")),
        name="gate_merge",
    )(xn, br, wg, wb)


def _mm_res_kernel(x_ref, w_ref, r_ref, o_ref):
    o_ref[...] = r_ref[...] + _dot(x_ref[...], w_ref[...])


def matmul_residual(x, w, res, *, tn=512, tm_target=640):
    m, k = x.shape
    n = w.shape[1]
    tm = _pick_tile(m, tm_target)
    return pl.pallas_call(
        _mm_res_kernel,
        grid=(m // tm, n // tn),
        in_specs=[pl.BlockSpec((tm, k), lambda i, j: (i, 0)),
                  pl.BlockSpec((k, tn), lambda i, j: (0, j)),
                  pl.BlockSpec((tm, tn), lambda i, j: (i, j))],
        out_specs=pl.BlockSpec((tm, tn), lambda i, j: (i, j)),
        out_shape=jax.ShapeDtypeStruct((m, n), F32),
        compiler_params=_cparams(("parallel", "parallel")),
        name="matmul_residual",
    )(x, w, res)


def _mm_kernel(x_ref, w_ref, o_ref):
    o_ref[...] = _dot(x_ref[...], w_ref[...]).astype(o_ref.dtype)


def matmul(x, w, *, tn, out_dtype=F32, tm_target=640):
    m, k = x.shape
    n = w.shape[1]
    tm = _pick_tile(m, tm_target)
    return pl.pallas_call(
        _mm_kernel,
        grid=(m // tm, n // tn),
        in_specs=[pl.BlockSpec((tm, k), lambda i, j: (i, 0)),
                  pl.BlockSpec((k, tn), lambda i, j: (0, j))],
        out_specs=pl.BlockSpec((tm, tn), lambda i, j: (i, j)),
        out_shape=jax.ShapeDtypeStruct((m, n), out_dtype),
        compiler_params=_cparams(("parallel", "parallel")),
        name="matmul",
    )(x, w)


def _ple_kernel(h_ref, g_ref, wg_ref, pe_ref, wp_ref, r_ref, o_ref, xs_ref):
    @pl.when(pl.program_id(1) == 0)
    def _():
        x = h_ref[...]
        ms = jnp.mean(x * x, axis=-1, keepdims=True)
        xs_ref[...] = (x * lax.rsqrt(ms + EPS) * g_ref[...]).astype(BF16)

    gate = jax.nn.sigmoid(_dot(xs_ref[...], wg_ref[...]))
    o_ref[...] = r_ref[...] + gate * _dot(pe_ref[...], wp_ref[...])


def ple_update(h, g, wg, pe, wp, *, tn=512, tm_target=640):
    m, k = h.shape
    kp = pe.shape[1]
    tm = _pick_tile(m, tm_target)
    return pl.pallas_call(
        _ple_kernel,
        grid=(m // tm, D_MODEL // tn),
        in_specs=[pl.BlockSpec((tm, k), lambda i, j: (i, 0)),
                  pl.BlockSpec((1, k), lambda i, j: (0, 0)),
                  pl.BlockSpec((k, tn), lambda i, j: (0, j)),
                  pl.BlockSpec((tm, kp), lambda i, j: (i, 0)),
                  pl.BlockSpec((kp, tn), lambda i, j: (0, j)),
                  pl.BlockSpec((tm, tn), lambda i, j: (i, j))],
        out_specs=pl.BlockSpec((tm, tn), lambda i, j: (i, j)),
        out_shape=jax.ShapeDtypeStruct((m, D_MODEL), F32),
        scratch_shapes=[pltpu.VMEM((tm, k), BF16)],
        compiler_params=_cparams(("parallel", "arbitrary")),
        name="ple_update",
    )(h, g.reshape(1, k), wg, pe, wp, h)


def _headmm_out_hm_kernel(x_ref, w_ref, o_ref, *, heads, din):
    for hh in range(heads):
        o_ref[hh] = _dot(x_ref[:, hh * din:(hh + 1) * din], w_ref[hh]).astype(o_ref.dtype)


def headmm_to_headmajor(x, w, *, out_dtype, tm_target=640):
    m = x.shape[0]
    heads, din, dout = w.shape
    tm = _pick_tile(m, tm_target)
    return pl.pallas_call(
        functools.partial(_headmm_out_hm_kernel, heads=heads, din=din),
        grid=(m // tm,),
        in_specs=[pl.BlockSpec((tm, heads * din), lambda i: (i, 0)),
                  pl.BlockSpec((heads, din, dout), lambda i: (0, 0, 0))],
        out_specs=pl.BlockSpec((heads, tm, dout), lambda i: (0, i, 0)),
        out_shape=jax.ShapeDtypeStruct((heads, m, dout), out_dtype),
        compiler_params=_cparams(("parallel",)),
        name="headmm_to_headmajor",
    )(x, w)


def _headmm_in_hm_kernel(x_ref, w_ref, o_ref, *, heads, dout):
    for hh in range(heads):
        o_ref[:, hh * dout:(hh + 1) * dout] = _dot(x_ref[hh], w_ref[hh]).astype(o_ref.dtype)


def headmm_from_headmajor(x, w, *, out_dtype, tm_target=640):
    heads, m, din = x.shape
    dout = w.shape[2]
    tm = _pick_tile(m, tm_target)
    return pl.pallas_call(
        functools.partial(_headmm_in_hm_kernel, heads=heads, dout=dout),
        grid=(m // tm,),
        in_specs=[pl.BlockSpec((heads, tm, din), lambda i: (0, i, 0)),
                  pl.BlockSpec((heads, din, dout), lambda i: (0, 0, 0))],
        out_specs=pl.BlockSpec((tm, heads * dout), lambda i: (i, 0)),
        out_shape=jax.ShapeDtypeStruct((m, heads * dout), out_dtype),
        compiler_params=_cparams(("parallel",)),
        name="headmm_from_headmajor",
    )(x, w)


def _rwkv_scan_kernel(r_ref, w_ref, k_ref, v_ref, nkk_ref, bb_ref, s0_ref, y_ref, sT_ref, s_ref, *, tc):
    c = pl.program_id(1)

    @pl.when(c == 0)
    def _():
        s_ref[...] = s0_ref[0]

    row = lax.broadcasted_iota(I32, (A_HD, A_HD), 0)
    col = lax.broadcasted_iota(I32, (A_HD, A_HD), 1)
    eye = (row == col).astype(F32)

    def step(t, carry):
        for hh in range(A_HEADS):
            sl = (0, hh, pl.ds(t, 1), slice(None))
            s_prev = s_ref[hh]
            sa = jnp.sum(s_prev * nkk_ref[sl], axis=1, keepdims=True)
            v_col = jnp.sum(eye * v_ref[sl], axis=1, keepdims=True)
            s_new = s_prev * w_ref[sl] + sa * bb_ref[sl] + v_col * k_ref[sl]
            s_ref[hh] = s_new
            y_col = jnp.sum(s_new * r_ref[sl], axis=1, keepdims=True)
            y_ref[sl] = jnp.sum(eye * y_col, axis=0, keepdims=True)
        return carry

    lax.fori_loop(0, tc, step, 0)

    @pl.when(c == pl.num_programs(1) - 1)
    def _():
        sT_ref[0] = s_ref[...]


def rwkv_scan(r, w, k, v, nkk, bb, s0):
    b, hds, t, n = r.shape
    tc = _pick_tile(t, 256)
    vec = pl.BlockSpec((1, hds, tc, n), lambda i, c: (i, 0, c, 0))
    st = pl.BlockSpec((1, hds, n, n), lambda i, c: (i, 0, 0, 0))
    return pl.pallas_call(
        functools.partial(_rwkv_scan_kernel, tc=tc),
        grid=(b, t // tc),
        in_specs=[vec, vec, vec, vec, vec, vec, st],
        out_specs=[vec, st],
        out_shape=[jax.ShapeDtypeStruct((b, hds, t, n), F32), jax.ShapeDtypeStruct((b, hds, n, n), F32)],
        scratch_shapes=[pltpu.VMEM((hds, n, n), F32)],
        compiler_params=_cparams(("parallel", "arbitrary")),
        name="rwkv_scan",
    )(r, w, k, v, nkk, bb, s0)


def _chunk_mlp_kernel(z_ref, lg_ref, lb_ref, ws_ref, bst_ref, o_ref):
    u = z_ref[0, :, :BR]
    x = z_ref[0, :, BR:]
    mu = jnp.mean(x, axis=-1, keepdims=True)
    var = jnp.mean(jnp.square(x - mu), axis=-1, keepdims=True)
    vn = (x - mu) * lax.rsqrt(var + EPS) * lg_ref[...] + lb_ref[...]
    tcn = ws_ref.shape[1]
    causal = lax.broadcasted_iota(I32, (tcn, tcn), 0) >= lax.broadcasted_iota(I32, (tcn, tcn), 1)
    for g in range(B_GROUPS):
        wmat = jnp.where(causal, ws_ref[g], 0.0).astype(BF16)
        s = _dot(wmat, vn[:, g * B_GC:(g + 1) * B_GC].astype(BF16)) + bst_ref[:, g:g + 1]
        o_ref[0, :, g * B_GC:(g + 1) * B_GC] = u[:, g * B_GC:(g + 1) * B_GC] * s


def chunk_mlp(zb, ln_g, ln_b, ws, bs):
    b, t, _ = zb.shape
    tc = min(t, CHUNK)
    return pl.pallas_call(
        _chunk_mlp_kernel,
        grid=(b, t // tc),
        in_specs=[pl.BlockSpec((1, tc, B_IN), lambda i, c: (i, c, 0)),
                  pl.BlockSpec((1, BR), lambda i, c: (0, 0)),
                  pl.BlockSpec((1, BR), lambda i, c: (0, 0)),
                  pl.BlockSpec((B_GROUPS, tc, tc), lambda i, c: (0, 0, 0)),
                  pl.BlockSpec((tc, B_GROUPS), lambda i, c: (0, 0))],
        out_specs=pl.BlockSpec((1, tc, BR), lambda i, c: (i, c, 0)),
        out_shape=jax.ShapeDtypeStruct((b, t, BR), F32),
        compiler_params=_cparams(("parallel", "parallel")),
        name="chunk_mlp",
    )(zb, ln_g.reshape(1, BR), ln_b.reshape(1, BR), ws[:, :tc, :tc], bs[:, :tc].T)


def _sortable_key(score):
    score = jnp.where(score == 0.0, 0.0, score)
    bits = pltpu.bitcast(score, I32)
    return jnp.where(bits < 0, bits ^ 0x7FFFFFFF, bits)


def _kth_threshold(count_ge, n_sel, shape):
    def body(i, cur):
        bit = lax.shift_left(jnp.int32(1), jnp.int32(31) - i)
        trial = cur | bit
        ok = count_ge(trial ^ INT_MIN) >= n_sel
        return jnp.where(ok, trial, cur)

    cur = lax.fori_loop(0, 32, body, jnp.zeros(shape, I32))
    return cur ^ INT_MIN


def _tie_cutoff(count_eq_le, need, n_pos, shape):
    nbits = max(1, (n_pos - 1).bit_length())

    def body(i, lohi):
        lo, hi = lohi
        mid = lax.shift_right_arithmetic(lo + hi, 1)
        ok = count_eq_le(mid) >= need
        return jnp.where(ok, lo, mid), jnp.where(ok, mid, hi)

    lo = jnp.full(shape, -1, I32)
    hi = jnp.full(shape, n_pos - 1, I32)
    _, hi = lax.fori_loop(0, nbits + 1, body, (lo, hi))
    return hi


def _dsa_prompt_kernel(q_ref, qi_ref, wi_ref, ki_ref, k_ref, v_ref, o_ref,
                       key_ref, m_ref, l_ref, acc_ref, *, t_total, wk, n_sel):
    qb = pl.program_id(1)
    tq = Q_BLOCK
    n_chunks = (qb * tq + tq - 1) // wk + 1
    t_pos = qb * tq + lax.broadcasted_iota(I32, (tq, wk), 0)
    s_off = lax.broadcasted_iota(I32, (tq, wk), 1)

    def score_body(c, carry):
        k0 = pl.multiple_of(c * wk, wk)
        ki = ki_ref[0, pl.ds(k0, wk), :]
        sc = jnp.zeros((tq, wk), F32)
        for hh in range(C_IDX_HEADS):
            e = _dot_nt(qi_ref[hh], ki) * (C_IDX_HD ** -0.5)
            sc = sc + wi_ref[:, hh:hh + 1] * jnp.maximum(e, 0.0)
        key = _sortable_key(sc)
        key_ref[c] = jnp.where(k0 + s_off <= t_pos, key, INT_MIN)
        return carry

    lax.fori_loop(0, n_chunks, score_body, 0)

    def count(pred):
        def body(c, acc):
            return acc + pred(c, key_ref[c]).astype(I32)
        acc = lax.fori_loop(0, n_chunks, body, jnp.zeros((tq, wk), I32))
        return jnp.sum(acc, axis=1, keepdims=True)

    thr = _kth_threshold(lambda th: count(lambda c, kk: kk >= th), n_sel, (tq, 1))
    n_gt = count(lambda c, kk: kk > thr)
    need = n_sel - n_gt
    cut = _tie_cutoff(
        lambda j: count(lambda c, kk: (kk == thr) & (c * wk + s_off <= j)), need, t_total, (tq, 1))
    tie_ok = thr != INT_MIN

    m_ref[...] = jnp.full(m_ref.shape, NEG, F32)
    l_ref[...] = jnp.zeros(l_ref.shape, F32)
    acc_ref[...] = jnp.zeros(acc_ref.shape, F32)
    grp = C_HEADS // C_KV_HEADS
    rows = grp * tq

    def attn_body(c, carry):
        k0 = pl.multiple_of(c * wk, wk)
        kk = key_ref[c]
        sel = (kk > thr) | ((kk == thr) & (k0 + s_off <= cut) & tie_ok)
        for n in range(C_KV_HEADS):
            qn = q_ref[n * grp:(n + 1) * grp].reshape(rows, C_HD)
            kn = k_ref[0, n, pl.ds(k0, wk), :]
            vn = v_ref[0, n, pl.ds(k0, wk), :]
            s = _dot_nt(qn, kn) * (C_HD ** -0.5)
            s = jnp.where(sel[None], s.reshape(grp, tq, wk), NEG).reshape(rows, wk)
            rs = slice(n * rows, (n + 1) * rows)
            m_old = m_ref[rs]
            m_new = jnp.maximum(m_old, jnp.max(s, axis=1, keepdims=True))
            p = jnp.exp(s - m_new)
            p = jnp.where(sel[None], p.reshape(grp, tq, wk), 0.0).reshape(rows, wk)
            alpha = jnp.exp(m_old - m_new)
            l_ref[rs] = alpha * l_ref[rs] + jnp.sum(p, axis=1, keepdims=True)
            acc_ref[rs] = alpha * acc_ref[rs] + _dot(p.astype(BF16), vn)
            m_ref[rs] = m_new
        return carry

    lax.fori_loop(0, n_chunks, attn_body, 0)
    o_ref[...] = (acc_ref[...] / l_ref[...]).reshape(C_HEADS, tq, C_HD)


def dsa_prompt(q, qi, wi, ki, k, v):
    b, t, _ = ki.shape
    tq = Q_BLOCK
    nq = t // tq
    wk = min(512, t)
    n_sel = min(C_TOPK, t // 4)
    kern = functools.partial(_dsa_prompt_kernel, t_total=t, wk=wk, n_sel=n_sel)
    return pl.pallas_call(
        kern,
        grid=(b, nq),
        in_specs=[pl.BlockSpec((C_HEADS, tq, C_HD), lambda i, j: (0, i * nq + j, 0)),
                  pl.BlockSpec((C_IDX_HEADS, tq, C_IDX_HD), lambda i, j: (0, i * nq + j, 0)),
                  pl.BlockSpec((tq, C_IDX_HEADS), lambda i, j: (i * nq + j, 0)),
                  pl.BlockSpec((1, t, C_IDX_HD), lambda i, j: (i, 0, 0)),
                  pl.BlockSpec((1, C_KV_HEADS, t, C_HD), lambda i, j: (i, 0, 0, 0)),
                  pl.BlockSpec((1, C_KV_HEADS, t, C_HD), lambda i, j: (i, 0, 0, 0))],
        out_specs=pl.BlockSpec((C_HEADS, tq, C_HD), lambda i, j: (0, i * nq + j, 0)),
        out_shape=jax.ShapeDtypeStruct((C_HEADS, b * t, C_HD), F32),
        scratch_shapes=[pltpu.VMEM((t // wk, tq, wk), I32),
                        pltpu.VMEM((C_HEADS * tq, 1), F32),
                        pltpu.VMEM((C_HEADS * tq, 1), F32),
                        pltpu.VMEM((C_HEADS * tq, C_HD), F32)],
        compiler_params=_cparams(("parallel", "arbitrary")),
        name="dsa_prompt",
    )(q, qi, wi, ki, k, v)


def _mla_prompt_kernel(qa_ref, qp_ref, ckv_ref, kpe_ref, o_ref, m_ref, l_ref, acc_ref, *, wk):
    qb = pl.program_id(1)
    tq = Q_BLOCK
    rows = D_HEADS * tq
    n_chunks = (qb * tq + tq - 1) // wk + 1
    qa = qa_ref[...].reshape(rows, D_KV_LORA)
    qp = qp_ref[...].reshape(rows, D_ROPE)
    t_pos = qb * tq + lax.broadcasted_iota(I32, (tq, wk), 0)
    s_off = lax.broadcasted_iota(I32, (tq, wk), 1)
    scale = (D_NOPE + D_ROPE) ** -0.5
    m_ref[...] = jnp.full(m_ref.shape, NEG, F32)
    l_ref[...] = jnp.zeros(l_ref.shape, F32)
    acc_ref[...] = jnp.zeros(acc_ref.shape, F32)

    def body(c, carry):
        k0 = pl.multiple_of(c * wk, wk)
        kc = ckv_ref[0, pl.ds(k0, wk), :]
        kp = kpe_ref[0, pl.ds(k0, wk), :]
        s = (_dot_nt(qa, kc) + _dot_nt(qp, kp)) * scale
        ok = (k0 + s_off <= t_pos)[None]
        s = jnp.where(ok, s.reshape(D_HEADS, tq, wk), NEG).reshape(rows, wk)
        m_old = m_ref[...]
        m_new = jnp.maximum(m_old, jnp.max(s, axis=1, keepdims=True))
        p = jnp.exp(s - m_new)
        alpha = jnp.exp(m_old - m_new)
        l_ref[...] = alpha * l_ref[...] + jnp.sum(p, axis=1, keepdims=True)
        acc_ref[...] = alpha * acc_ref[...] + _dot(p.astype(BF16), kc)
        m_ref[...] = m_new
        return carry

    lax.fori_loop(0, n_chunks, body, 0)
    o_ref[...] = (acc_ref[...] / l_ref[...]).reshape(D_HEADS, tq, D_KV_LORA).astype(o_ref.dtype)


def mla_prompt(qa, qp, ckv, kpe):
    b, t, _ = ckv.shape
    tq = Q_BLOCK
    nq = t // tq
    wk = min(512, t)
    return pl.pallas_call(
        functools.partial(_mla_prompt_kernel, wk=wk),
        grid=(b, nq),
        in_specs=[pl.BlockSpec((D_HEADS, tq, D_KV_LORA), lambda i, j: (0, i * nq + j, 0)),
                  pl.BlockSpec((D_HEADS, tq, D_ROPE), lambda i, j: (0, i * nq + j, 0)),
                  pl.BlockSpec((1, t, D_KV_LORA), lambda i, j: (i, 0, 0)),
                  pl.BlockSpec((1, t, D_ROPE), lambda i, j: (i, 0, 0))],
        out_specs=pl.BlockSpec((D_HEADS, tq, D_KV_LORA), lambda i, j: (0, i * nq + j, 0)),
        out_shape=jax.ShapeDtypeStruct((D_HEADS, b * t, D_KV_LORA), BF16),
        scratch_shapes=[pltpu.VMEM((D_HEADS * tq, 1), F32),
                        pltpu.VMEM((D_HEADS * tq, 1), F32),
                        pltpu.VMEM((D_HEADS * tq, D_KV_LORA), F32)],
        compiler_params=_cparams(("parallel", "arbitrary")),
        name="mla_prompt",
    )(qa, qp, ckv, kpe)


def _page_copy(pool_ref, buf_ref, sem_ref, page, slot, p, which):
    return pltpu.make_async_copy(
        pool_ref.at[page],
        buf_ref.at[slot, pl.ds(p * PAGE_SIZE, PAGE_SIZE)],
        sem_ref.at[slot, which])


def _fetch_pages(pt_ref, pools, bufs, sem_ref, seq, slot, n_pages, wait):
    def body(p, carry):
        page = 0 if wait else pt_ref[seq, p]
        for which, (pool_ref, buf_ref) in enumerate(zip(pools, bufs)):
            cp = _page_copy(pool_ref, buf_ref, sem_ref, page, slot, p, which)
            if wait:
                cp.wait()
            else:
                cp.start()
        return carry

    lax.fori_loop(0, n_pages, body, 0)


def _paged_prologue(pt_ref, pools, bufs, sem_ref, n_pages):
    seq = pl.program_id(0)
    nseq = pl.num_programs(0)
    slot = seq % 2

    @pl.when(seq == 0)
    def _():
        _fetch_pages(pt_ref, pools, bufs, sem_ref, seq, slot, n_pages, wait=False)

    @pl.when(seq + 1 < nseq)
    def _():
        _fetch_pages(pt_ref, pools, bufs, sem_ref, seq + 1, 1 - slot, n_pages, wait=False)

    _fetch_pages(pt_ref, pools, bufs, sem_ref, seq, slot, n_pages, wait=True)
    return slot


def _dsa_decode_kernel(pt_ref, q_ref, qi_ref, wi_ref, kin_ref, kn_ref, vn_ref,
                       kidx_pool, k_pool, v_pool, o_ref,
                       kidx_buf, k_buf, v_buf, sem_ref, *, n_pages, n_sel):
    past = n_pages * PAGE_SIZE
    slot = _paged_prologue(pt_ref, (kidx_pool, k_pool, v_pool), (kidx_buf, k_buf, v_buf), sem_ref, n_pages)

    qi = qi_ref[0]
    wi = wi_ref[0]
    e_past = _dot_nt(qi, kidx_buf[slot].astype(BF16)) * (C_IDX_HD ** -0.5)
    sc_past = jnp.sum(wi * jnp.maximum(e_past, 0.0), axis=0, keepdims=True)
    kin = kin_ref[0].astype(BF16).astype(F32)
    e_new = jnp.sum(qi.astype(F32) * kin, axis=1, keepdims=True) * (C_IDX_HD ** -0.5)
    sc_new = jnp.sum(wi * jnp.maximum(e_new, 0.0), axis=0, keepdims=True)
    key_past = _sortable_key(sc_past)
    key_new = _sortable_key(sc_new)
    pos = lax.broadcasted_iota(I32, (1, past), 1)

    def count_ge(th):
        return (jnp.sum((key_past >= th).astype(I32), axis=1, keepdims=True)
                + (key_new >= th).astype(I32))

    thr = _kth_threshold(count_ge, n_sel, (1, 1))
    n_gt = jnp.sum((key_past > thr).astype(I32), axis=1, keepdims=True) + (key_new > thr).astype(I32)
    need = n_sel - n_gt
    eq_past = key_past == thr
    cut = _tie_cutoff(
        lambda j: jnp.sum((eq_past & (pos <= j)).astype(I32), axis=1, keepdims=True), need, past, (1, 1))
    n_eq_past = jnp.sum(eq_past.astype(I32), axis=1, keepdims=True)
    sel_past = (key_past > thr) | (eq_past & (pos <= cut))
    sel_new = (key_new > thr) | ((key_new == thr) & (n_eq_past < need))

    grp = C_HEADS // C_KV_HEADS
    for n in range(C_KV_HEADS):
        qn = q_ref[0, n * grp:(n + 1) * grp]
        lanes = slice(n * C_HD, (n + 1) * C_HD)
        kp = k_buf[slot, :, lanes].astype(BF16)
        vp = v_buf[slot, :, lanes].astype(BF16)
        s = jnp.where(sel_past, _dot_nt(qn, kp) * (C_HD ** -0.5), NEG)
        k_new = kn_ref[0, :, lanes].astype(BF16).astype(F32)
        v_new = vn_ref[0, :, lanes].astype(BF16).astype(F32)
        s_new = jnp.sum(qn.astype(F32) * k_new, axis=1, keepdims=True) * (C_HD ** -0.5)
        s_new = jnp.where(sel_new, s_new, NEG)
        m = jnp.maximum(jnp.max(s, axis=1, keepdims=True), s_new)
        p = jnp.where(sel_past, jnp.exp(s - m), 0.0)
        p_new = jnp.where(sel_new, jnp.exp(s_new - m), 0.0)
        denom = jnp.sum(p, axis=1, keepdims=True) + p_new
        num = _dot(p.astype(BF16), vp) + p_new.astype(BF16).astype(F32) * v_new
        o_ref[0, n * grp:(n + 1) * grp] = num / denom


def dsa_decode(page_table, q, qi, wi, ki_new, k_new, v_new, kidx_pool, k_pool, v_pool):
    b, n_pages = page_table.shape
    past = n_pages * PAGE_SIZE
    n_sel = min(C_TOPK, (past + 1) // 4)
    kvw = C_KV_HEADS * C_HD
    any_spec = pl.BlockSpec(memory_space=pl.ANY)
    grid_spec = pltpu.PrefetchScalarGridSpec(
        num_scalar_prefetch=1,
        grid=(b,),
        in_specs=[pl.BlockSpec((1, C_HEADS, C_HD), lambda i, pt: (i, 0, 0)),
                  pl.BlockSpec((1, C_IDX_HEADS, C_IDX_HD), lambda i, pt: (i, 0, 0)),
                  pl.BlockSpec((1, C_IDX_HEADS, 1), lambda i, pt: (i, 0, 0)),
                  pl.BlockSpec((1, 1, C_IDX_HD), lambda i, pt: (i, 0, 0)),
                  pl.BlockSpec((1, 1, kvw), lambda i, pt: (i, 0, 0)),
                  pl.BlockSpec((1, 1, kvw), lambda i, pt: (i, 0, 0)),
                  any_spec, any_spec, any_spec],
        out_specs=pl.BlockSpec((1, C_HEADS, C_HD), lambda i, pt: (i, 0, 0)),
        scratch_shapes=[pltpu.VMEM((2, past, C_IDX_HD), F32),
                        pltpu.VMEM((2, past, kvw), F32),
                        pltpu.VMEM((2, past, kvw), F32),
                        pltpu.SemaphoreType.DMA((2, 3))])
    return pl.pallas_call(
        functools.partial(_dsa_decode_kernel, n_pages=n_pages, n_sel=n_sel),
        grid_spec=grid_spec,
        out_shape=jax.ShapeDtypeStruct((b, C_HEADS, C_HD), F32),
        compiler_params=_cparams(("arbitrary",)),
        name="dsa_decode",
    )(page_table, q, qi, wi, ki_new, k_new, v_new, kidx_pool, k_pool, v_pool)


def _mla_decode_kernel(pt_ref, qa_ref, qp_ref, cn_ref, pn_ref, ckv_pool, kpe_pool, o_ref,
                       ckv_buf, kpe_buf, sem_ref, *, n_pages):
    slot = _paged_prologue(pt_ref, (ckv_pool, kpe_pool), (ckv_buf, kpe_buf), sem_ref, n_pages)
    scale = (D_NOPE + D_ROPE) ** -0.5
    qa = qa_ref[0]
    qp = qp_ref[0]
    ckv = ckv_buf[slot].astype(BF16)
    kpe = kpe_buf[slot].astype(BF16)
    s = (_dot_nt(qa, ckv) + _dot_nt(qp, kpe)) * scale
    c_new = cn_ref[0].astype(BF16).astype(F32)
    p_new_k = pn_ref[0].astype(BF16).astype(F32)
    s_new = (jnp.sum(qa.astype(F32) * c_new, axis=1, keepdims=True)
             + jnp.sum(qp.astype(F32) * p_new_k, axis=1, keepdims=True)) * scale
    m = jnp.maximum(jnp.max(s, axis=1, keepdims=True), s_new)
    p = jnp.exp(s - m)
    p_new = jnp.exp(s_new - m)
    denom = jnp.sum(p, axis=1, keepdims=True) + p_new
    num = _dot(p.astype(BF16), ckv) + p_new.astype(BF16).astype(F32) * c_new
    o_ref[0] = (num / denom).astype(o_ref.dtype)


def mla_decode(page_table, qa, qp, ckv_new, kpe_new, ckv_pool, kpe_pool):
    b, n_pages = page_table.shape
    past = n_pages * PAGE_SIZE
    any_spec = pl.BlockSpec(memory_space=pl.ANY)
    grid_spec = pltpu.PrefetchScalarGridSpec(
        num_scalar_prefetch=1,
        grid=(b,),
        in_specs=[pl.BlockSpec((1, D_HEADS, D_KV_LORA), lambda i, pt: (i, 0, 0)),
                  pl.BlockSpec((1, D_HEADS, D_ROPE), lambda i, pt: (i, 0, 0)),
                  pl.BlockSpec((1, 1, D_KV_LORA), lambda i, pt: (i, 0, 0)),
                  pl.BlockSpec((1, 1, D_ROPE), lambda i, pt: (i, 0, 0)),
                  any_spec, any_spec],
        out_specs=pl.BlockSpec((1, D_HEADS, D_KV_LORA), lambda i, pt: (i, 0, 0)),
        scratch_shapes=[pltpu.VMEM((2, past, D_KV_LORA), F32),
                        pltpu.VMEM((2, past, D_ROPE), F32),
                        pltpu.SemaphoreType.DMA((2, 2))])
    return pl.pallas_call(
        functools.partial(_mla_decode_kernel, n_pages=n_pages),
        grid_spec=grid_spec,
        out_shape=jax.ShapeDtypeStruct((b, D_HEADS, D_KV_LORA), BF16),
        compiler_params=_cparams(("arbitrary",)),
        name="mla_decode",
    )(page_table, qa, qp, ckv_new, kpe_new, ckv_pool, kpe_pool)


def _rmsnorm(x, g, eps=EPS):
    return x * lax.rsqrt(jnp.mean(x * x, axis=-1, keepdims=True) + eps) * g


def _rope(x, pos):
    half = x.shape[-1] // 2
    inv = ROPE_THETA ** (-jnp.arange(half, dtype=F32) / half)
    ang = pos.astype(F32)[:, None] * inv[None, :]
    shape = (x.shape[0],) + (1,) * (x.ndim - 2) + (half,)
    cos = jnp.cos(ang).reshape(shape)
    sin = jnp.sin(ang).reshape(shape)
    x1, x2 = x[..., :half], x[..., half:]
    return jnp.concatenate([x1 * cos - x2 * sin, x2 * cos + x1 * sin], axis=-1)


def _heads_major(x, heads):
    m = x.shape[0]
    return jnp.transpose(x.reshape(m, heads, -1), (1, 0, 2))


def _layer(h, pe, prm, cache, page_table, state, dims):
    bp, tp, bs = dims
    p_rows = bp * tp
    m = h.shape[0]
    pos = jnp.concatenate([jnp.tile(jnp.arange(tp, dtype=I32), bp),
                           jnp.full((bs,), page_table.shape[1] * PAGE_SIZE, I32)])

    xn, z = norm_matmul(h, prm['norm_mix_g'], prm['w_mix'], tn=640, emit_xn=True)
    o1 = A_IN
    o2 = o1 + B_IN
    o3 = o2 + C_PAD
    za, zb, zc, zd = z[:, :o1], z[:, o1:o2], z[:, o2:o2 + C_IN], z[:, o3:o3 + D_IN]

    za_p = za[:p_rows].reshape(bp, tp, A_IN)
    prev_p = jnp.concatenate([jnp.zeros((bp, 1, A_IN), F32), za_p[:, :-1]], axis=1).reshape(p_rows, A_IN)
    z_prev = jnp.concatenate([prev_p, state['shift']], axis=0)
    zs = za + (z_prev - za) * prm['a_mu']
    r, k, v = zs[:, :BR], zs[:, BR:2 * BR], zs[:, 2 * BR:3 * BR]
    xw, xa = zs[:, 3 * BR:3 * BR + A_DECAY_LORA], zs[:, 3 * BR + A_DECAY_LORA:]
    lora = matmul(jnp.concatenate([jnp.tanh(xw), xa], axis=1).astype(BF16), prm['a_lora'], tn=2 * BR)
    decay = jnp.exp(-DECAY_SCALE * jax.nn.sigmoid(prm['a_w0'] + lora[:, :BR]))
    a = jax.nn.sigmoid(prm['a_a0'] + lora[:, BR:])
    hd = lambda x: x.reshape(m, A_HEADS, A_HD)
    kk = hd(k * prm['a_kk'])
    kk = kk / jnp.maximum(jnp.sqrt(jnp.sum(kk * kk, axis=-1, keepdims=True)), 1e-12)
    k2 = hd(k * (1.0 + (a - 1.0) * prm['a_ka']))
    r3, v3, w3, a3 = hd(r), hd(v), hd(decay), hd(a)
    bonus = jnp.sum(r3 * k2 * prm['a_rk'], axis=-1, keepdims=True) * v3

    def scan_group(sl, b, t, s0):
        tr = lambda x: jnp.transpose(x[sl].reshape(b, t, A_HEADS, A_HD), (0, 2, 1, 3))
        y, s_t = rwkv_scan(tr(r3), tr(w3), tr(k2), tr(v3), tr(-kk), tr(kk * a3), s0)
        return jnp.transpose(y, (0, 2, 1, 3)).reshape(b * t, A_HEADS, A_HD), s_t

    y_p, s_p = scan_group(slice(0, p_rows), bp, tp, jnp.zeros((bp, A_HEADS, A_HD, A_HD), F32))
    y_s, s_s = scan_group(slice(p_rows, m), bs, 1, state['rwkv'])
    y = jnp.concatenate([y_p, y_s], axis=0)
    mu = jnp.mean(y, axis=-1, keepdims=True)
    var = jnp.mean(jnp.square(y - mu), axis=-1, keepdims=True)
    y = ((y - mu) * lax.rsqrt(var + A_LN_EPS) * prm['a_lnx_g'].reshape(A_HEADS, A_HD)
         + prm['a_lnx_b'].reshape(A_HEADS, A_HD))
    o_a = (y + bonus).reshape(m, BR)
    shift_p = za_p[:, -1]
    shift_s = za[p_rows:]

    ob_p = chunk_mlp(zb[:p_rows].reshape(bp, tp, B_IN), prm['b_ln_g'], prm['b_ln_b'],
                     prm['b_ws'], prm['b_bs']).reshape(p_rows, BR)
    zb_s = zb[p_rows:]
    xs = zb_s[:, BR:]
    mu = jnp.mean(xs, axis=-1, keepdims=True)
    var = jnp.mean(jnp.square(xs - mu), axis=-1, keepdims=True)
    v_rows = (xs - mu) * lax.rsqrt(var + EPS) * prm['b_ln_g'] + prm['b_ln_b']
    w00 = jnp.repeat(prm['b_ws'][:, 0, 0], B_GC)
    b00 = jnp.repeat(prm['b_bs'][:, 0], B_GC)
    ob_s = zb_s[:, :BR] * (v_rows.astype(BF16).astype(F32) * w00.astype(BF16).astype(F32) + b00)
    o_b = jnp.concatenate([ob_p, ob_s], axis=0)

    c1 = C_HEADS * C_HD
    c2 = c1 + C_KV_HEADS * C_HD
    c3 = c2 + C_KV_HEADS * C_HD
    c4 = c3 + C_IDX_HEADS * C_IDX_HD
    c5 = c4 + C_IDX_HD
    q = _rope(_rmsnorm(zc[:, :c1].reshape(m, C_HEADS, C_HD), prm['c_qn_g']), pos)
    kc = _rope(_rmsnorm(zc[:, c1:c2].reshape(m, C_KV_HEADS, C_HD), prm['c_kn_g']), pos)
    vc = zc[:, c2:c3].reshape(m, C_KV_HEADS, C_HD)
    qi = _rope(zc[:, c3:c4].reshape(m, C_IDX_HEADS, C_IDX_HD), pos)
    ki = _rope(zc[:, c4:c5], pos)
    wi = zc[:, c5:] * (C_IDX_HEADS ** -0.5)
    q_b, qi_b = q.astype(BF16), qi.astype(BF16)
    kvp = lambda x: jnp.transpose(x[:p_rows].reshape(bp, tp, C_KV_HEADS, C_HD), (0, 2, 1, 3)).astype(BF16)
    oc_p = dsa_prompt(jnp.transpose(q_b[:p_rows], (1, 0, 2)), jnp.transpose(qi_b[:p_rows], (1, 0, 2)),
                      wi[:p_rows], ki[:p_rows].reshape(bp, tp, C_IDX_HD).astype(BF16), kvp(kc), kvp(vc))
    oc_p = jnp.transpose(oc_p, (1, 0, 2)).reshape(p_rows, BR)
    kvw = C_KV_HEADS * C_HD
    oc_s = dsa_decode(page_table, q_b[p_rows:], qi_b[p_rows:], wi[p_rows:, :, None],
                      ki[p_rows:, None, :], kc[p_rows:].reshape(bs, 1, kvw), vc[p_rows:].reshape(bs, 1, kvw),
                      cache['kidx'], cache['k'].reshape(-1, PAGE_SIZE, kvw), cache['v'].reshape(-1, PAGE_SIZE, kvw))
    o_c = jnp.concatenate([oc_p, oc_s.reshape(bs, BR)], axis=0)

    d1 = D_Q_LORA
    d2 = d1 + D_KV_LORA
    cq = _rmsnorm(zd[:, :d1], prm['d_qa_norm_g'])
    qd = matmul(cq.astype(BF16), prm['d_w_uq'], tn=D_HEADS * (D_NOPE + D_ROPE))
    qd = _rmsnorm(qd.reshape(m, D_HEADS, D_NOPE + D_ROPE), prm['d_qn_g'])
    q_nope = qd[..., :D_NOPE].reshape(m, D_HEADS * D_NOPE)
    q_pe = jnp.transpose(_rope(qd[..., D_NOPE:], pos), (1, 0, 2)).astype(BF16)
    ckv = _rmsnorm(zd[:, d1:d2], prm['d_kva_norm_g'])
    kpe = _rope(_rmsnorm(zd[:, d2:], prm['d_kn_g']), pos)
    q_abs = headmm_to_headmajor(q_nope.astype(BF16), prm['d_w_uk'], out_dtype=BF16)
    ol_p = mla_prompt(q_abs[:, :p_rows], q_pe[:, :p_rows],
                      ckv[:p_rows].reshape(bp, tp, D_KV_LORA).astype(BF16),
                      kpe[:p_rows].reshape(bp, tp, D_ROPE).astype(BF16))
    ol_s = mla_decode(page_table, jnp.transpose(q_abs[:, p_rows:], (1, 0, 2)),
                      jnp.transpose(q_pe[:, p_rows:], (1, 0, 2)),
                      ckv[p_rows:, None, :], kpe[p_rows:, None, :], cache['ckv'], cache['kpe'])
    o_lat = jnp.concatenate([ol_p, jnp.transpose(ol_s, (1, 0, 2))], axis=1)
    o_d = headmm_from_headmajor(o_lat, prm['d_w_uv'], out_dtype=BF16)

    br = jnp.stack([o_a.astype(BF16), o_b.astype(BF16), o_c.astype(BF16), o_d], axis=0)
    merged = gate_merge(xn, br, prm['w_gate'], prm['w_branch'])
    h = matmul_residual(merged, prm['w_out'], h)

    up = norm_matmul(h, prm['norm_ffn_g'], prm['f_up'], tn=512, emit_xn=False)
    up_p = up[:p_rows].reshape(bp, tp, -1)
    full_p = jnp.concatenate([jnp.zeros((bp, 2, up.shape[1]), F32), up_p], axis=1)
    full_s = jnp.concatenate([state['conv'], up[p_rows:, None, :]], axis=1)
    cw, cb = prm['f_cw'], prm['f_cb']

    def conv(full, t):
        c = cb + full[:, 0:t] * cw[0]
        for j in range(1, 3):
            c = c + full[:, j:j + t] * cw[j]
        dff = c.shape[-1] // 2
        return (jax.nn.silu(c[..., :dff]) * c[..., dff:]).reshape(-1, dff)

    act = jnp.concatenate([conv(full_p, tp), conv(full_s, 1)], axis=0).astype(BF16)
    h = matmul_residual(act, prm['f_down'], h)

    h = ple_update(h, prm['norm_ple_g'], prm['w_ple_gate'], pe, prm['w_ple_proj'])

    st_p = (s_p, shift_p, full_p[:, -2:], kc[:p_rows].reshape(bp, tp, C_KV_HEADS, C_HD),
            vc[:p_rows].reshape(bp, tp, C_KV_HEADS, C_HD), ki[:p_rows].reshape(bp, tp, C_IDX_HD),
            ckv[:p_rows].reshape(bp, tp, D_KV_LORA), kpe[:p_rows].reshape(bp, tp, D_ROPE))
    st_s = (s_s, shift_s, full_s[:, -2:], kc[p_rows:].reshape(bs, 1, C_KV_HEADS, C_HD),
            vc[p_rows:].reshape(bs, 1, C_KV_HEADS, C_HD), ki[p_rows:].reshape(bs, 1, C_IDX_HD),
            ckv[p_rows:].reshape(bs, 1, D_KV_LORA), kpe[p_rows:].reshape(bs, 1, D_ROPE),
            v_rows.reshape(bs, 1, BR))
    return h, st_p, st_s


def kernel(x_prompt, x_sample, state_rwkv, state_rwkv_shift, state_ffn_conv, cache_dsa_k, cache_dsa_v, cache_dsa_kidx, cache_mla_ckv, cache_mla_kpe, page_table, p_prompt, p_sample, norm_mix_g, w_in, a_mu, a_w0, a_w2, a_a0, a_a2, a_kk, a_ka, a_rk, a_lnx_g, a_lnx_b, b_ln_g, b_ln_b, b_ws, b_bs, c_qn_g, c_kn_g, d_qa_norm_g, d_w_uq, d_qn_g, d_kva_norm_g, d_kn_g, d_w_uk, d_w_uv, w_branch, w_out, norm_ffn_g, f_up, f_cw, f_cb, f_down, norm_ple_g, w_ple_gate, w_ple_proj):
    bp, tp, _ = x_prompt.shape
    bs = x_sample.shape[0]
    depth = w_in.shape[0]
    p_rows = bp * tp
    h = jnp.concatenate([x_prompt.reshape(p_rows, D_MODEL), x_sample.reshape(bs, D_MODEL)], axis=0)
    o1 = A_IN
    o2 = o1 + B_IN
    o3 = o2 + C_IN
    o4 = o3 + D_IN
    sp, ss = [], []
    for i in range(depth):
        wi_ = w_in[i]
        pad = lambda x, n: jnp.pad(x, ((0, 0), (0, n - x.shape[1])))
        w_mix = jnp.concatenate([wi_[:, :o2], pad(wi_[:, o2:o3], C_PAD), pad(wi_[:, o3:o4], D_PAD)], axis=1)
        zeros = jnp.zeros((A_DECAY_LORA, BR), F32)
        a_lora = jnp.concatenate([jnp.concatenate([a_w2[i], zeros], axis=1),
                                  jnp.concatenate([zeros, a_a2[i]], axis=1)], axis=0)
        prm = dict(
            norm_mix_g=norm_mix_g[i], w_mix=w_mix.astype(BF16), w_gate=wi_[:, o4:].astype(BF16),
            a_mu=a_mu[i], a_w0=a_w0[i], a_a0=a_a0[i], a_lora=a_lora.astype(BF16),
            a_kk=a_kk[i], a_ka=a_ka[i], a_rk=a_rk[i], a_lnx_g=a_lnx_g[i], a_lnx_b=a_lnx_b[i],
            b_ln_g=b_ln_g[i], b_ln_b=b_ln_b[i], b_ws=b_ws[i], b_bs=b_bs[i],
            c_qn_g=c_qn_g[i], c_kn_g=c_kn_g[i],
            d_qa_norm_g=d_qa_norm_g[i], d_w_uq=d_w_uq[i].astype(BF16), d_qn_g=d_qn_g[i],
            d_kva_norm_g=d_kva_norm_g[i], d_kn_g=d_kn_g[i],
            d_w_uk=jnp.transpose(d_w_uk[i], (1, 2, 0)).astype(BF16),
            d_w_uv=jnp.transpose(d_w_uv[i], (1, 0, 2)).astype(BF16),
            w_branch=w_branch[i].astype(BF16), w_out=w_out[i].astype(BF16),
            norm_ffn_g=norm_ffn_g[i], f_up=f_up[i].astype(BF16), f_cw=f_cw[i], f_cb=f_cb[i],
            f_down=f_down[i].astype(BF16), norm_ple_g=norm_ple_g[i],
            w_ple_gate=w_ple_gate[i].astype(BF16), w_ple_proj=w_ple_proj[i].astype(BF16))
        pe = jnp.concatenate([p_prompt[i].reshape(p_rows, -1), p_sample[i].reshape(bs, -1)], axis=0).astype(BF16)
        cache = dict(k=cache_dsa_k[i], v=cache_dsa_v[i], kidx=cache_dsa_kidx[i],
                     ckv=cache_mla_ckv[i], kpe=cache_mla_kpe[i])
        state = dict(rwkv=state_rwkv[i], shift=state_rwkv_shift[i], conv=state_ffn_conv[i])
        h, st_p, st_s = _layer(h, pe, prm, cache, page_table, state, (bp, tp, bs))
        sp.append(st_p)
        ss.append(st_s)

    def stk(lst, j):
        return jnp.stack([s[j] for s in lst], axis=0)

    return ((h[:p_rows].reshape(bp, tp, D_MODEL), h[p_rows:].reshape(bs, 1, D_MODEL))
            + tuple(stk(sp, j) for j in range(8)) + tuple(stk(ss, j) for j in range(9)))
```

```python
import functools

import jax
import jax.numpy as jnp
from jax import lax
from jax.experimental import pallas as pl
from jax.experimental.pallas import tpu as pltpu

F32 = jnp.float32
BF16 = jnp.bfloat16
I32 = jnp.int32

D_MODEL = 2048
N_BRANCH = 4
BR = D_MODEL // N_BRANCH
EPS = 1e-6
ROPE_THETA = 10000.0
Q_BLOCK = 128
A_HD = 64
A_HEADS = BR // A_HD
A_DECAY_LORA = 64
A_AAA_LORA = 64
A_IN = 3 * BR + A_DECAY_LORA + A_AAA_LORA
A_LN_EPS = 64e-5
DECAY_SCALE = 0.606531
CHUNK = 128
B_GROUPS = 4
B_GC = BR // B_GROUPS
B_IN = 2 * BR
C_HD = 64
C_HEADS = BR // C_HD
C_KV_HEADS = 2
C_IDX_HEADS = 4
C_IDX_HD = 64
C_TOPK = 256
C_IN = C_HEADS * C_HD + 2 * C_KV_HEADS * C_HD + C_IDX_HEADS * C_IDX_HD + C_IDX_HD + C_IDX_HEADS
D_NOPE = 64
D_ROPE = 32
D_V = 64
D_HEADS = BR // D_V
D_Q_LORA = 3 * D_MODEL // 16
D_KV_LORA = D_MODEL // 16
D_IN = D_Q_LORA + D_KV_LORA + D_ROPE
PAGE_SIZE = 128

LANE = 128
C_PAD = 1152
D_PAD = 640
N_MIX = A_IN + B_IN + C_PAD + D_PAD
VMEM_LIMIT = 56 * 1024 * 1024
KEY_CHUNK = 512
DEC_SEQS = 8
NEG = -1e30
INT_MIN = -2147483648


def _pick_tile(n, target, mult=8):
    best = n
    for t in range(mult, min(n, target) + 1, mult):
        if n % t == 0:
            best = t
    return best


def _cparams(sem):
    return pltpu.CompilerParams(dimension_semantics=sem, vmem_limit_bytes=VMEM_LIMIT)


def _rowmax(x):
    part = x[:, 0:LANE]
    for u in range(1, x.shape[1] // LANE):
        part = jnp.maximum(part, x[:, u * LANE:(u + 1) * LANE])
    return jnp.max(part, axis=1, keepdims=True)


def _rowsum(x):
    part = x[:, 0:LANE]
    for u in range(1, x.shape[1] // LANE):
        part = part + x[:, u * LANE:(u + 1) * LANE]
    return jnp.sum(part, axis=1, keepdims=True)


def _dot(a, b):
    return jnp.dot(a, b, preferred_element_type=F32)


def _dot_nt(a, b):
    return lax.dot_general(a, b, (((1,), (1,)), ((), ())), preferred_element_type=F32)


def _rms_rows(h_ref, g_ref):
    x = h_ref[...]
    ms = jnp.mean(x * x, axis=-1, keepdims=True)
    return (x * lax.rsqrt(ms + EPS) * g_ref[...]).astype(BF16)


def _norm_mm_kernel(h_ref, g_ref, w_ref, *rest, emit_xn):
    if emit_xn:
        xn_ref, z_ref, xs_ref = rest
    else:
        z_ref, xs_ref = rest

    @pl.when(pl.program_id(1) == 0)
    def _():
        xn = _rms_rows(h_ref, g_ref)
        xs_ref[...] = xn
        if emit_xn:
            xn_ref[...] = xn

    z_ref[...] = _dot(xs_ref[...], w_ref[...])


def norm_matmul(h, g, w, layer, *, tn, emit_xn, tm_target=640):
    m, k = h.shape
    n = w.shape[2]
    tm = _pick_tile(m, tm_target)
    out_shape = [jax.ShapeDtypeStruct((m, n), F32)]
    out_specs = [pl.BlockSpec((tm, tn), lambda i, j: (i, j))]
    if emit_xn:
        out_shape.insert(0, jax.ShapeDtypeStruct((m, k), BF16))
        out_specs.insert(0, pl.BlockSpec((tm, k), lambda i, j: (i, 0)))
    outs = pl.pallas_call(
        functools.partial(_norm_mm_kernel, emit_xn=emit_xn),
        grid=(m // tm, n // tn),
        in_specs=[pl.BlockSpec((tm, k), lambda i, j: (i, 0)),
                  pl.BlockSpec((None, 1, k), lambda i, j: (layer, 0, 0)),
                  pl.BlockSpec((None, k, tn), lambda i, j: (layer, 0, j))],
        out_specs=out_specs,
        out_shape=out_shape,
        scratch_shapes=[pltpu.VMEM((tm, k), BF16)],
        compiler_params=_cparams(("parallel", "arbitrary")),
        name="norm_matmul",
    )(h, g, w)
    return outs if emit_xn else outs[0]


def _gate_merge_kernel(xn_ref, br_ref, wg_ref, wb_ref, o_ref, acc_ref):
    kk = pl.program_id(2)
    gate = jax.nn.sigmoid(_dot(xn_ref[...], wg_ref[...]))
    up = _dot(br_ref[0], wb_ref[...])
    contrib = gate * up

    @pl.when(kk == 0)
    def _():
        acc_ref[...] = contrib

    @pl.when(kk > 0)
    def _():
        acc_ref[...] += contrib

    @pl.when(kk == N_BRANCH - 1)
    def _():
        o_ref[...] = acc_ref[...].astype(BF16)


def gate_merge(xn, br, wg, wb, layer, *, tn=512, tm_target=640):
    m, k = xn.shape
    tm = _pick_tile(m, tm_target)
    nj = D_MODEL // tn
    return pl.pallas_call(
        _gate_merge_kernel,
        grid=(m // tm, nj, N_BRANCH),
        in_specs=[pl.BlockSpec((tm, k), lambda i, j, c: (i, 0)),
                  pl.BlockSpec((1, tm, BR), lambda i, j, c: (c, i, 0)),
                  pl.BlockSpec((None, k, tn), lambda i, j, c: (layer, 0, c * nj + j)),
                  pl.BlockSpec((None, None, BR, tn), lambda i, j, c: (layer, c, 0, j))],
        out_specs=pl.BlockSpec((tm, tn), lambda i, j, c: (i, j)),
        out_shape=jax.ShapeDtypeStruct((m, D_MODEL), BF16),
        scratch_shapes=[pltpu.VMEM((tm, tn), F32)],
        compiler_params=_cparams(("parallel", "parallel", "arbitrary")),
        name="gate_merge",
    )(xn, br, wg, wb)


def _mm_res_kernel(x_ref, w_ref, r_ref, o_ref):
    o_ref[...] = r_ref[...] + _dot(x_ref[...], w_ref[...])


def matmul_residual(x, w, layer, res, *, tn=512, tm_target=640):
    m, k = x.shape
    n = w.shape[2]
    tm = _pick_tile(m, tm_target)
    return pl.pallas_call(
        _mm_res_kernel,
        grid=(m // tm, n // tn),
        in_specs=[pl.BlockSpec((tm, k), lambda i, j: (i, 0)),
                  pl.BlockSpec((None, k, tn), lambda i, j: (layer, 0, j)),
                  pl.BlockSpec((tm, tn), lambda i, j: (i, j))],
        out_specs=pl.BlockSpec((tm, tn), lambda i, j: (i, j)),
        out_shape=jax.ShapeDtypeStruct((m, n), F32),
        compiler_params=_cparams(("parallel", "parallel")),
        name="matmul_residual",
    )(x, w, res)


def _mm_kernel(x_ref, w_ref, o_ref):
    o_ref[...] = _dot(x_ref[...], w_ref[...]).astype(o_ref.dtype)


def matmul(x, w, layer, *, tn, out_dtype=F32, tm_target=640):
    m, k = x.shape
    n = w.shape[2]
    tm = _pick_tile(m, tm_target)
    return pl.pallas_call(
        _mm_kernel,
        grid=(m // tm, n // tn),
        in_specs=[pl.BlockSpec((tm, k), lambda i, j: (i, 0)),
                  pl.BlockSpec((None, k, tn), lambda i, j: (layer, 0, j))],
        out_specs=pl.BlockSpec((tm, tn), lambda i, j: (i, j)),
        out_shape=jax.ShapeDtypeStruct((m, n), out_dtype),
        compiler_params=_cparams(("parallel", "parallel")),
        name="matmul",
    )(x, w)


def _ple_kernel(h_ref, g_ref, wg_ref, pe_ref, wp_ref, r_ref, o_ref, xs_ref):
    @pl.when(pl.program_id(1) == 0)
    def _():
        xs_ref[...] = _rms_rows(h_ref, g_ref)

    gate = jax.nn.sigmoid(_dot(xs_ref[...], wg_ref[...]))
    o_ref[...] = r_ref[...] + gate * _dot(pe_ref[...], wp_ref[...])


def ple_update(h, g, wg, pe, wp, layer, *, tn=512, tm_target=640):
    m, k = h.shape
    kp = pe.shape[2]
    tm = _pick_tile(m, tm_target)
    return pl.pallas_call(
        _ple_kernel,
        grid=(m // tm, D_MODEL // tn),
        in_specs=[pl.BlockSpec((tm, k), lambda i, j: (i, 0)),
                  pl.BlockSpec((None, 1, k), lambda i, j: (layer, 0, 0)),
                  pl.BlockSpec((None, k, tn), lambda i, j: (layer, 0, j)),
                  pl.BlockSpec((None, tm, kp), lambda i, j: (layer, i, 0)),
                  pl.BlockSpec((None, kp, tn), lambda i, j: (layer, 0, j)),
                  pl.BlockSpec((tm, tn), lambda i, j: (i, j))],
        out_specs=pl.BlockSpec((tm, tn), lambda i, j: (i, j)),
        out_shape=jax.ShapeDtypeStruct((m, D_MODEL), F32),
        scratch_shapes=[pltpu.VMEM((tm, k), BF16)],
        compiler_params=_cparams(("parallel", "arbitrary")),
        name="ple_update",
    )(h, g, wg, pe, wp, h)


def _headmm_out_hm_kernel(x_ref, w_ref, o_ref, *, heads, din):
    for hh in range(heads):
        o_ref[hh] = _dot(x_ref[:, hh * din:(hh + 1) * din], w_ref[hh]).astype(o_ref.dtype)


def headmm_to_headmajor(x, w, layer, *, out_dtype, tm_target=640):
    m = x.shape[0]
    _, heads, din, dout = w.shape
    tm = _pick_tile(m, tm_target)
    return pl.pallas_call(
        functools.partial(_headmm_out_hm_kernel, heads=heads, din=din),
        grid=(m // tm,),
        in_specs=[pl.BlockSpec((tm, heads * din), lambda i: (i, 0)),
                  pl.BlockSpec((None, heads, din, dout), lambda i: (layer, 0, 0, 0))],
        out_specs=pl.BlockSpec((heads, tm, dout), lambda i: (0, i, 0)),
        out_shape=jax.ShapeDtypeStruct((heads, m, dout), out_dtype),
        compiler_params=_cparams(("parallel",)),
        name="headmm_to_headmajor",
    )(x, w)


def _headmm_in_hm_kernel(x_ref, w_ref, o_ref, *, heads, dout):
    for hh in range(heads):
        o_ref[:, hh * dout:(hh + 1) * dout] = _dot(x_ref[hh], w_ref[hh]).astype(o_ref.dtype)


def headmm_from_headmajor(x, w, layer, *, out_dtype, tm_target=640):
    heads, m, din = x.shape
    dout = w.shape[3]
    tm = _pick_tile(m, tm_target)
    return pl.pallas_call(
        functools.partial(_headmm_in_hm_kernel, heads=heads, dout=dout),
        grid=(m // tm,),
        in_specs=[pl.BlockSpec((heads, tm, din), lambda i: (0, i, 0)),
                  pl.BlockSpec((None, heads, din, dout), lambda i: (layer, 0, 0, 0))],
        out_specs=pl.BlockSpec((tm, heads * dout), lambda i: (i, 0)),
        out_shape=jax.ShapeDtypeStruct((m, heads * dout), out_dtype),
        compiler_params=_cparams(("parallel",)),
        name="headmm_from_headmajor",
    )(x, w)


def _conv_swiglu_kernel(ua_ref, ub_ref, ha_ref, hb_ref, sa_ref, sb_ref, cwa_ref, cwb_ref, cba_ref, cbb_ref,
                        o_ref, *, tm, t_prompt, n_prompt_blocks):
    i = pl.program_id(0)
    is_sample = i >= n_prompt_blocks
    seq_start = (i * tm) % t_prompt == 0
    r = lax.broadcasted_iota(I32, ua_ref.shape, 0)

    def conv(u_ref, h_ref, s_ref, cw_ref, cb_ref):
        x = u_ref[...]
        h1 = jnp.where(seq_start, 0.0, h_ref[7:8, :])
        h2 = jnp.where(seq_start, 0.0, h_ref[6:7, :])
        x1 = jnp.where(r >= 1, pltpu.roll(x, 1, axis=0), h1)
        x2 = jnp.where(r >= 2, pltpu.roll(x, 2, axis=0), jnp.where(r == 1, h1, h2))
        x1 = jnp.where(is_sample, s_ref[1], x1)
        x2 = jnp.where(is_sample, s_ref[0], x2)
        return cb_ref[...] + x2 * cw_ref[0:1, :] + x1 * cw_ref[1:2, :] + x * cw_ref[2:3, :]

    ca = conv(ua_ref, ha_ref, sa_ref, cwa_ref, cba_ref)
    cb = conv(ub_ref, hb_ref, sb_ref, cwb_ref, cbb_ref)
    o_ref[...] = (ca * jax.nn.sigmoid(ca) * cb).astype(o_ref.dtype)


def conv_swiglu(up, conv0, cw, cb, layer, *, t_prompt, n_sample, tn=512):
    m, f2 = up.shape
    f = f2 // 2
    tm = _pick_tile(n_sample, 128)
    p_rows = m - n_sample
    assert p_rows % tm == 0 and t_prompt % tm == 0 and tm % 8 == 0
    npb = p_rows // tm
    nj = f // tn
    hb = tm // 8
    main_a = pl.BlockSpec((tm, tn), lambda i, j: (i, j))
    main_b = pl.BlockSpec((tm, tn), lambda i, j: (i, j + nj))
    halo_a = pl.BlockSpec((8, tn), lambda i, j: (jnp.maximum(i * hb - 1, 0), j))
    halo_b = pl.BlockSpec((8, tn), lambda i, j: (jnp.maximum(i * hb - 1, 0), j + nj))
    st_a = pl.BlockSpec((2, tm, tn), lambda i, j: (0, jnp.maximum(i - npb, 0), j))
    st_b = pl.BlockSpec((2, tm, tn), lambda i, j: (0, jnp.maximum(i - npb, 0), j + nj))
    cw_a = pl.BlockSpec((None, 3, tn), lambda i, j: (layer, 0, j))
    cw_b = pl.BlockSpec((None, 3, tn), lambda i, j: (layer, 0, j + nj))
    cb_a = pl.BlockSpec((None, 1, tn), lambda i, j: (layer, 0, j))
    cb_b = pl.BlockSpec((None, 1, tn), lambda i, j: (layer, 0, j + nj))
    return pl.pallas_call(
        functools.partial(_conv_swiglu_kernel, tm=tm, t_prompt=t_prompt, n_prompt_blocks=npb),
        grid=(m // tm, nj),
        in_specs=[main_a, main_b, halo_a, halo_b, st_a, st_b, cw_a, cw_b, cb_a, cb_b],
        out_specs=pl.BlockSpec((tm, tn), lambda i, j: (i, j)),
        out_shape=jax.ShapeDtypeStruct((m, f), BF16),
        compiler_params=_cparams(("parallel", "parallel")),
        name="conv_swiglu",
    )(up, up, up, up, conv0, conv0, cw, cw, cb, cb)


def _rwkv_scan_kernel(r_ref, w_ref, k_ref, v_ref, nkk_ref, bb_ref, s0_ref, y_ref, sT_ref, s_ref, *, tc, nb):
    c = pl.program_id(1)

    @pl.when(c == 0)
    def _():
        s_ref[...] = s0_ref[...]

    row = lax.broadcasted_iota(I32, (A_HD, A_HD), 0)
    col = lax.broadcasted_iota(I32, (A_HD, A_HD), 1)
    eye = (row == col).astype(F32)

    def step(t, carry):
        sl = (slice(None), slice(None), pl.ds(t, 1), slice(None))
        s_prev = s_ref[...]
        sa = jnp.sum(s_prev * nkk_ref[sl], axis=-1, keepdims=True)
        v_col = jnp.sum(eye * v_ref[sl], axis=-1, keepdims=True)
        s_new = s_prev * w_ref[sl] + sa * bb_ref[sl] + v_col * k_ref[sl]
        s_ref[...] = s_new
        y_col = jnp.sum(s_new * r_ref[sl], axis=-1, keepdims=True)
        y_ref[sl] = jnp.sum(eye * y_col, axis=-2, keepdims=True)
        return carry

    lax.fori_loop(0, tc, step, 0)

    @pl.when(c == pl.num_programs(1) - 1)
    def _():
        sT_ref[...] = s_ref[...]


def rwkv_scan(r, w, k, v, nkk, bb, s0):
    b, hds, t, n = r.shape
    tc = _pick_tile(t, 256)
    nb = 2 if b % 2 == 0 else 1
    vec = pl.BlockSpec((nb, hds, tc, n), lambda i, c: (i, 0, c, 0))
    st = pl.BlockSpec((nb, hds, n, n), lambda i, c: (i, 0, 0, 0))
    return pl.pallas_call(
        functools.partial(_rwkv_scan_kernel, tc=tc, nb=nb),
        grid=(b // nb, t // tc),
        in_specs=[vec, vec, vec, vec, vec, vec, st],
        out_specs=[vec, st],
        out_shape=[jax.ShapeDtypeStruct((b, hds, t, n), F32), jax.ShapeDtypeStruct((b, hds, n, n), F32)],
        scratch_shapes=[pltpu.VMEM((nb, hds, n, n), F32)],
        compiler_params=_cparams(("parallel", "arbitrary")),
        name="rwkv_scan",
    )(r, w, k, v, nkk, bb, s0)


def _chunk_mlp_kernel(z_ref, lg_ref, lb_ref, ws_ref, bst_ref, o_ref):
    u = z_ref[0, :, :BR]
    x = z_ref[0, :, BR:]
    mu = jnp.mean(x, axis=-1, keepdims=True)
    var = jnp.mean(jnp.square(x - mu), axis=-1, keepdims=True)
    vn = (x - mu) * lax.rsqrt(var + EPS) * lg_ref[...] + lb_ref[...]
    tcn = ws_ref.shape[1]
    causal = lax.broadcasted_iota(I32, (tcn, tcn), 0) >= lax.broadcasted_iota(I32, (tcn, tcn), 1)
    for g in range(B_GROUPS):
        wmat = jnp.where(causal, ws_ref[g], 0.0).astype(BF16)
        s = _dot(wmat, vn[:, g * B_GC:(g + 1) * B_GC].astype(BF16)) + bst_ref[:, g:g + 1]
        o_ref[0, :, g * B_GC:(g + 1) * B_GC] = u[:, g * B_GC:(g + 1) * B_GC] * s


def chunk_mlp(zb, ln_g, ln_b, ws, bs):
    b, t, _ = zb.shape
    tc = min(t, CHUNK)
    return pl.pallas_call(
        _chunk_mlp_kernel,
        grid=(b, t // tc),
        in_specs=[pl.BlockSpec((1, tc, B_IN), lambda i, c: (i, c, 0)),
                  pl.BlockSpec((1, BR), lambda i, c: (0, 0)),
                  pl.BlockSpec((1, BR), lambda i, c: (0, 0)),
                  pl.BlockSpec((B_GROUPS, tc, tc), lambda i, c: (0, 0, 0)),
                  pl.BlockSpec((tc, B_GROUPS), lambda i, c: (0, 0))],
        out_specs=pl.BlockSpec((1, tc, BR), lambda i, c: (i, c, 0)),
        out_shape=jax.ShapeDtypeStruct((b, t, BR), F32),
        compiler_params=_cparams(("parallel", "parallel")),
        name="chunk_mlp",
    )(zb, ln_g.reshape(1, BR), ln_b.reshape(1, BR), ws[:, :tc, :tc], bs[:, :tc].T)


def _sortable_key(score):
    score = jnp.where(score == 0.0, 0.0, score)
    bits = pltpu.bitcast(score, I32)
    return jnp.where(bits < 0, bits ^ 0x7FFFFFFF, bits)


def _count_rows(key_ref, n_chunks, rows, wk, pred):
    def body(c, acc):
        hit = pred(c, key_ref[c]).astype(I32)
        part = hit[:, 0:LANE]
        for u in range(1, wk // LANE):
            part = part + hit[:, u * LANE:(u + 1) * LANE]
        return acc + part

    acc = lax.fori_loop(0, n_chunks, body, jnp.zeros((rows, LANE), I32))
    return jnp.sum(acc, axis=1, keepdims=True)


def _select_threshold(key_ref, n_chunks, rows, wk, n_sel, n_pos):
    s_off = lax.broadcasted_iota(I32, (rows, wk), 1)

    def bit_body(i, cur):
        bit = lax.shift_left(jnp.int32(1), jnp.int32(31) - i)
        trial = cur | bit
        th = trial ^ INT_MIN
        ok = _count_rows(key_ref, n_chunks, rows, wk, lambda c, kk: kk >= th) >= n_sel
        return jnp.where(ok, trial, cur)

    thr = lax.fori_loop(0, 32, bit_body, jnp.zeros((rows, 1), I32)) ^ INT_MIN
    n_ge = _count_rows(key_ref, n_chunks, rows, wk, lambda c, kk: kk >= thr)
    n_gt = _count_rows(key_ref, n_chunks, rows, wk, lambda c, kk: kk > thr)
    need = n_sel - n_gt
    tie_rows = jnp.where((n_ge > n_sel) & (thr != INT_MIN), 1, 0)
    any_tie = jnp.max(tie_rows)

    def tie_search():
        nbits = max(1, (n_pos - 1).bit_length())

        def body(i, lohi):
            lo, hi = lohi
            mid = lax.shift_right_arithmetic(lo + hi, 1)
            cnt = _count_rows(key_ref, n_chunks, rows, wk,
                              lambda c, kk: (kk == thr) & (c * wk + s_off <= mid))
            ok = cnt >= need
            return jnp.where(ok, lo, mid), jnp.where(ok, mid, hi)

        lo = jnp.full((rows, 1), -1, I32)
        hi = jnp.full((rows, 1), n_pos - 1, I32)
        return lax.fori_loop(0, nbits + 1, body, (lo, hi))[1]

    cut = lax.cond(any_tie > 0, tie_search, lambda: jnp.full((rows, 1), n_pos - 1, I32))
    return thr, cut


def _dsa_prompt_kernel(q_ref, qi_ref, wi_ref, ki_ref, k_ref, v_ref, o_ref,
                       key_ref, m_ref, l_ref, acc_ref, *, t_total, wk, n_sel):
    qb = pl.program_id(1)
    tq = Q_BLOCK
    n_chunks = (qb * tq + tq - 1) // wk + 1
    t_pos = qb * tq + lax.broadcasted_iota(I32, (tq, wk), 0)
    s_off = lax.broadcasted_iota(I32, (tq, wk), 1)

    def score_body(c, carry):
        k0 = pl.multiple_of(c * wk, wk)
        ki = ki_ref[0, pl.ds(k0, wk), :]
        sc = jnp.zeros((tq, wk), F32)
        for hh in range(C_IDX_HEADS):
            e = _dot_nt(qi_ref[hh], ki) * (C_IDX_HD ** -0.5)
            sc = sc + wi_ref[:, hh:hh + 1] * jnp.maximum(e, 0.0)
        key_ref[c] = jnp.where(k0 + s_off <= t_pos, _sortable_key(sc), INT_MIN)
        return carry

    lax.fori_loop(0, n_chunks, score_body, 0)
    thr, cut = _select_threshold(key_ref, n_chunks, tq, wk, n_sel, t_total)
    tie_ok = thr != INT_MIN

    m_ref[...] = jnp.full(m_ref.shape, NEG, F32)
    l_ref[...] = jnp.zeros(l_ref.shape, F32)
    acc_ref[...] = jnp.zeros(acc_ref.shape, F32)
    grp = C_HEADS // C_KV_HEADS
    rows = grp * tq

    def attn_body(c, carry):
        k0 = pl.multiple_of(c * wk, wk)
        kk = key_ref[c]
        sel = ((kk > thr) | ((kk == thr) & (k0 + s_off <= cut) & tie_ok))[None]
        for n in range(C_KV_HEADS):
            qn = q_ref[n * grp:(n + 1) * grp].reshape(rows, C_HD)
            kn = k_ref[0, n, pl.ds(k0, wk), :]
            vn = v_ref[0, n, pl.ds(k0, wk), :]
            s = _dot_nt(qn, kn) * (C_HD ** -0.5)
            s = jnp.where(sel, s.reshape(grp, tq, wk), NEG).reshape(rows, wk)
            rs = slice(n * rows, (n + 1) * rows)
            m_old = m_ref[rs]
            m_new = jnp.maximum(m_old, _rowmax(s))
            p = jnp.exp(s - jnp.tile(m_new, (1, wk // LANE)))
            alpha = jnp.exp(m_old - m_new)
            l_ref[rs] = alpha * l_ref[rs] + _rowsum(p)
            acc_ref[rs] = alpha[:, :C_HD] * acc_ref[rs] + _dot(p.astype(BF16), vn)
            m_ref[rs] = m_new
        return carry

    lax.fori_loop(0, n_chunks, attn_body, 0)
    o_ref[...] = (acc_ref[...] / l_ref[:, :C_HD]).reshape(C_HEADS, tq, C_HD)


def dsa_prompt(q, qi, wi, ki, k, v):
    b, t, _ = ki.shape
    tq = Q_BLOCK
    nq = t // tq
    wk = min(KEY_CHUNK, t)
    n_sel = min(C_TOPK, t // 4)
    kern = functools.partial(_dsa_prompt_kernel, t_total=t, wk=wk, n_sel=n_sel)
    return pl.pallas_call(
        kern,
        grid=(b, nq),
        in_specs=[pl.BlockSpec((C_HEADS, tq, C_HD), lambda i, j: (0, i * nq + j, 0)),
                  pl.BlockSpec((C_IDX_HEADS, tq, C_IDX_HD), lambda i, j: (0, i * nq + j, 0)),
                  pl.BlockSpec((tq, C_IDX_HEADS), lambda i, j: (i * nq + j, 0)),
                  pl.BlockSpec((1, t, C_IDX_HD), lambda i, j: (i, 0, 0)),
                  pl.BlockSpec((1, C_KV_HEADS, t, C_HD), lambda i, j: (i, 0, 0, 0)),
                  pl.BlockSpec((1, C_KV_HEADS, t, C_HD), lambda i, j: (i, 0, 0, 0))],
        out_specs=pl.BlockSpec((C_HEADS, tq, C_HD), lambda i, j: (0, i * nq + j, 0)),
        out_shape=jax.ShapeDtypeStruct((C_HEADS, b * t, C_HD), F32),
        scratch_shapes=[pltpu.VMEM((t // wk, tq, wk), I32),
                        pltpu.VMEM((C_HEADS * tq, LANE), F32),
                        pltpu.VMEM((C_HEADS * tq, LANE), F32),
                        pltpu.VMEM((C_HEADS * tq, C_HD), F32)],
        compiler_params=_cparams(("parallel", "arbitrary")),
        name="dsa_prompt",
    )(q, qi, wi, ki, k, v)


def _mla_prompt_kernel(qa_ref, qp_ref, ckv_ref, kpe_ref, o_ref, m_ref, l_ref, acc_ref, *, wk):
    qb = pl.program_id(1)
    tq = Q_BLOCK
    rows = D_HEADS * tq
    n_chunks = (qb * tq + tq - 1) // wk + 1
    qa = qa_ref[...].reshape(rows, D_KV_LORA)
    qp = qp_ref[...].reshape(rows, D_ROPE)
    t_pos = qb * tq + lax.broadcasted_iota(I32, (tq, wk), 0)
    s_off = lax.broadcasted_iota(I32, (tq, wk), 1)
    scale = (D_NOPE + D_ROPE) ** -0.5
    m_ref[...] = jnp.full(m_ref.shape, NEG, F32)
    l_ref[...] = jnp.zeros(l_ref.shape, F32)
    acc_ref[...] = jnp.zeros(acc_ref.shape, F32)

    def body(c, carry):
        k0 = pl.multiple_of(c * wk, wk)
        kc = ckv_ref[0, pl.ds(k0, wk), :]
        kp = kpe_ref[0, pl.ds(k0, wk), :]
        s = (_dot_nt(qa, kc) + _dot_nt(qp, kp)) * scale
        ok = (k0 + s_off <= t_pos)[None]
        s = jnp.where(ok, s.reshape(D_HEADS, tq, wk), NEG).reshape(rows, wk)
        m_old = m_ref[...]
        m_new = jnp.maximum(m_old, _rowmax(s))
        p = jnp.exp(s - jnp.tile(m_new, (1, wk // LANE)))
        alpha = jnp.exp(m_old - m_new)
        l_ref[...] = alpha * l_ref[...] + _rowsum(p)
        acc_ref[...] = alpha * acc_ref[...] + _dot(p.astype(BF16), kc)
        m_ref[...] = m_new
        return carry

    lax.fori_loop(0, n_chunks, body, 0)
    o_ref[...] = (acc_ref[...] / l_ref[...]).reshape(D_HEADS, tq, D_KV_LORA).astype(o_ref.dtype)


def mla_prompt(qa, qp, ckv, kpe):
    b, t, _ = ckv.shape
    tq = Q_BLOCK
    nq = t // tq
    wk = min(KEY_CHUNK, t)
    return pl.pallas_call(
        functools.partial(_mla_prompt_kernel, wk=wk),
        grid=(b, nq),
        in_specs=[pl.BlockSpec((D_HEADS, tq, D_KV_LORA), lambda i, j: (0, i * nq + j, 0)),
                  pl.BlockSpec((D_HEADS, tq, D_ROPE), lambda i, j: (0, i * nq + j, 0)),
                  pl.BlockSpec((1, t, D_KV_LORA), lambda i, j: (i, 0, 0)),
                  pl.BlockSpec((1, t, D_ROPE), lambda i, j: (i, 0, 0))],
        out_specs=pl.BlockSpec((D_HEADS, tq, D_KV_LORA), lambda i, j: (0, i * nq + j, 0)),
        out_shape=jax.ShapeDtypeStruct((D_HEADS, b * t, D_KV_LORA), BF16),
        scratch_shapes=[pltpu.VMEM((D_HEADS * tq, LANE), F32),
                        pltpu.VMEM((D_HEADS * tq, LANE), F32),
                        pltpu.VMEM((D_HEADS * tq, D_KV_LORA), F32)],
        compiler_params=_cparams(("parallel", "arbitrary")),
        name="mla_prompt",
    )(qa, qp, ckv, kpe)


def _page_copy(pool_ref, buf_ref, sem_ref, layer, page, slot, row0, which):
    return pltpu.make_async_copy(
        pool_ref.at[layer, page],
        buf_ref.at[slot, pl.ds(row0, PAGE_SIZE)],
        sem_ref.at[slot, which])


def _fetch_pages(pt_ref, pools, bufs, sem_ref, layer, seq, slot, n_pages, wait):
    def body(p, carry):
        page = 0 if wait else pt_ref[seq, p]
        for which, (pool_ref, buf_ref) in enumerate(zip(pools, bufs)):
            cp = _page_copy(pool_ref, buf_ref, sem_ref, layer, page, slot, p * PAGE_SIZE, which)
            if wait:
                cp.wait()
            else:
                cp.start()
        return carry

    lax.fori_loop(0, n_pages, body, 0)


def _paged_prologue(pt_ref, pools, bufs, sem_ref, layer, n_pages, seq, n_seq):
    slot = seq % 2
    args = (pt_ref, pools, bufs, sem_ref, layer)

    @pl.when(seq == 0)
    def _():
        _fetch_pages(*args, seq, slot, n_pages, wait=False)

    @pl.when(seq + 1 < n_seq)
    def _():
        _fetch_pages(*args, seq + 1, 1 - slot, n_pages, wait=False)

    _fetch_pages(*args, seq, slot, n_pages, wait=True)
    return slot


def _dsa_decode_keys_kernel(pt_ref, qi_ref, wi_ref, kin_ref, kidx_pool, key_ref, kidx_buf, sem_ref,
                            *, layer, n_pages, wk):
    past = n_pages * PAGE_SIZE
    n_seq = pl.num_programs(0) * DEC_SEQS
    lane = lax.broadcasted_iota(I32, (1, wk), 1)
    for s in range(DEC_SEQS):
        seq = pl.program_id(0) * DEC_SEQS + s
        slot = _paged_prologue(pt_ref, (kidx_pool,), (kidx_buf,), sem_ref, layer, n_pages, seq, n_seq)
        qi = qi_ref[s]
        wi = wi_ref[s]
        kidx = kidx_buf[slot].astype(BF16)
        e_past = _dot_nt(qi, kidx) * (C_IDX_HD ** -0.5)
        key_past = _sortable_key(jnp.sum(wi * jnp.maximum(e_past, 0.0), axis=0, keepdims=True))
        kin = kin_ref[s].astype(BF16).astype(F32)
        e_new = jnp.sum(qi.astype(F32) * kin, axis=1, keepdims=True) * (C_IDX_HD ** -0.5)
        key_new = _sortable_key(jnp.sum(wi * jnp.maximum(e_new, 0.0), axis=0, keepdims=True))
        for c in range(past // wk):
            key_ref[c, s:s + 1, :] = key_past[:, c * wk:(c + 1) * wk]
        key_ref[past // wk, s:s + 1, :] = jnp.where(lane == 0, key_new, INT_MIN)


def dsa_decode_keys(page_table, qi, wi, ki_new, kidx_pool, layer):
    b, n_pages = page_table.shape
    past = n_pages * PAGE_SIZE
    wk = min(KEY_CHUNK, past)
    n_chunks = past // wk + 1
    grid_spec = pltpu.PrefetchScalarGridSpec(
        num_scalar_prefetch=1,
        grid=(b // DEC_SEQS,),
        in_specs=[pl.BlockSpec((DEC_SEQS, C_IDX_HEADS, C_IDX_HD), lambda i, pt: (i, 0, 0)),
                  pl.BlockSpec((DEC_SEQS, C_IDX_HEADS, 1), lambda i, pt: (i, 0, 0)),
                  pl.BlockSpec((DEC_SEQS, 1, C_IDX_HD), lambda i, pt: (i, 0, 0)),
                  pl.BlockSpec(memory_space=pl.ANY)],
        out_specs=pl.BlockSpec((n_chunks, DEC_SEQS, wk), lambda i, pt: (0, i, 0)),
        scratch_shapes=[pltpu.VMEM((2, past, C_IDX_HD), F32),
                        pltpu.SemaphoreType.DMA((2, 1))])
    return pl.pallas_call(
        functools.partial(_dsa_decode_keys_kernel, layer=layer, n_pages=n_pages, wk=wk),
        grid_spec=grid_spec,
        out_shape=jax.ShapeDtypeStruct((n_chunks, b, wk), I32),
        compiler_params=_cparams(("arbitrary",)),
        name="dsa_decode_keys",
    )(page_table, qi, wi, ki_new, kidx_pool)


def _dsa_select_kernel(key_ref, thr_ref, cut_ref, *, n_sel):
    n_chunks, rows, wk = key_ref.shape
    thr, cut = _select_threshold(key_ref, n_chunks, rows, wk, n_sel, n_chunks * wk)
    thr_ref[...] = thr
    cut_ref[...] = cut


def dsa_select(keys, n_sel):
    n_chunks, b, wk = keys.shape
    rows = _pick_tile(b, 128)
    return pl.pallas_call(
        functools.partial(_dsa_select_kernel, n_sel=n_sel),
        grid=(b // rows,),
        in_specs=[pl.BlockSpec((n_chunks, rows, wk), lambda i: (0, i, 0))],
        out_specs=[pl.BlockSpec((rows, 1), lambda i: (i, 0)), pl.BlockSpec((rows, 1), lambda i: (i, 0))],
        out_shape=[jax.ShapeDtypeStruct((b, 1), I32), jax.ShapeDtypeStruct((b, 1), I32)],
        compiler_params=_cparams(("parallel",)),
        name="dsa_select",
    )(keys)


def _dsa_decode_attn_kernel(pt_ref, q_ref, key_ref, thr_ref, cut_ref, kn_ref, vn_ref, k_pool, v_pool, o_ref,
                            k_buf, v_buf, sem_ref, *, layer, n_pages):
    past = n_pages * PAGE_SIZE
    slot = _paged_prologue(pt_ref, (k_pool, v_pool), (k_buf, v_buf), sem_ref, layer, n_pages,
                           pl.program_id(0), pl.num_programs(0))
    thr = thr_ref[0]
    cut = cut_ref[0]
    tie_ok = thr != INT_MIN
    key_past = key_ref[0, :, :past]
    key_new = key_ref[0, :, past:past + 1]
    pos = lax.broadcasted_iota(I32, (1, past), 1)
    sel_past = (key_past > thr) | ((key_past == thr) & (pos <= cut) & tie_ok)
    sel_new = (key_new > thr) | ((key_new == thr) & (cut >= past) & tie_ok)

    grp = C_HEADS // C_KV_HEADS
    for n in range(C_KV_HEADS):
        qn = q_ref[0, n * grp:(n + 1) * grp]
        lanes = slice(n * C_HD, (n + 1) * C_HD)
        kp = k_buf[slot, :, lanes].astype(BF16)
        vp = v_buf[slot, :, lanes].astype(BF16)
        s = jnp.where(sel_past, _dot_nt(qn, kp) * (C_HD ** -0.5), NEG)
        k_new = kn_ref[0, :, lanes].astype(BF16).astype(F32)
        v_new = vn_ref[0, :, lanes].astype(BF16).astype(F32)
        s_new = jnp.sum(qn.astype(F32) * k_new, axis=1, keepdims=True) * (C_HD ** -0.5)
        s_new = jnp.where(sel_new, s_new, NEG)
        m = jnp.maximum(jnp.max(s, axis=1, keepdims=True), s_new)
        p = jnp.where(sel_past, jnp.exp(s - m), 0.0)
        p_new = jnp.where(sel_new, jnp.exp(s_new - m), 0.0)
        denom = jnp.sum(p, axis=1, keepdims=True) + p_new
        num = _dot(p.astype(BF16), vp) + p_new.astype(BF16).astype(F32) * v_new
        o_ref[0, n * grp:(n + 1) * grp] = num / denom


def dsa_decode_attn(page_table, q, key_rows, thr, cut, k_new, v_new, k_pool, v_pool, layer):
    b, n_pages = page_table.shape
    past = n_pages * PAGE_SIZE
    n_pos = key_rows.shape[2]
    kvw = C_KV_HEADS * C_HD
    any_spec = pl.BlockSpec(memory_space=pl.ANY)
    grid_spec = pltpu.PrefetchScalarGridSpec(
        num_scalar_prefetch=1,
        grid=(b,),
        in_specs=[pl.BlockSpec((1, C_HEADS, C_HD), lambda i, pt: (i, 0, 0)),
                  pl.BlockSpec((1, 1, n_pos), lambda i, pt: (i, 0, 0)),
                  pl.BlockSpec((1, 1, 1), lambda i, pt: (i, 0, 0)),
                  pl.BlockSpec((1, 1, 1), lambda i, pt: (i, 0, 0)),
                  pl.BlockSpec((1, 1, kvw), lambda i, pt: (i, 0, 0)),
                  pl.BlockSpec((1, 1, kvw), lambda i, pt: (i, 0, 0)),
                  any_spec, any_spec],
        out_specs=pl.BlockSpec((1, C_HEADS, C_HD), lambda i, pt: (i, 0, 0)),
        scratch_shapes=[pltpu.VMEM((2, past, kvw), F32),
                        pltpu.VMEM((2, past, kvw), F32),
                        pltpu.SemaphoreType.DMA((2, 2))])
    return pl.pallas_call(
        functools.partial(_dsa_decode_attn_kernel, layer=layer, n_pages=n_pages),
        grid_spec=grid_spec,
        out_shape=jax.ShapeDtypeStruct((b, C_HEADS, C_HD), F32),
        compiler_params=_cparams(("arbitrary",)),
        name="dsa_decode_attn",
    )(page_table, q, key_rows, thr, cut, k_new, v_new, k_pool, v_pool)


def _mla_decode_kernel(pt_ref, qa_ref, qp_ref, cn_ref, pn_ref, ckv_pool, kpe_pool, o_ref,
                       ckv_buf, kpe_buf, sem_ref, *, layer, n_pages):
    slot = _paged_prologue(pt_ref, (ckv_pool, kpe_pool), (ckv_buf, kpe_buf), sem_ref, layer, n_pages,
                           pl.program_id(0), pl.num_programs(0))
    scale = (D_NOPE + D_ROPE) ** -0.5
    qa = qa_ref[0]
    qp = qp_ref[0]
    ckv = ckv_buf[slot].astype(BF16)
    kpe = kpe_buf[slot].astype(BF16)
    s = (_dot_nt(qa, ckv) + _dot_nt(qp, kpe)) * scale
    c_new = cn_ref[0].astype(BF16).astype(F32)
    p_new_k = pn_ref[0].astype(BF16).astype(F32)
    s_new = (jnp.sum(qa.astype(F32) * c_new, axis=1, keepdims=True)
             + jnp.sum(qp.astype(F32) * p_new_k, axis=1, keepdims=True)) * scale
    m = jnp.maximum(jnp.max(s, axis=1, keepdims=True), s_new)
    p = jnp.exp(s - m)
    p_new = jnp.exp(s_new - m)
    denom = jnp.sum(p, axis=1, keepdims=True) + p_new
    num = _dot(p.astype(BF16), ckv) + p_new.astype(BF16).astype(F32) * c_new
    o_ref[0] = (num / denom).astype(o_ref.dtype)


def mla_decode(page_table, qa, qp, ckv_new, kpe_new, ckv_pool, kpe_pool, layer):
    b, n_pages = page_table.shape
    past = n_pages * PAGE_SIZE
    any_spec = pl.BlockSpec(memory_space=pl.ANY)
    grid_spec = pltpu.PrefetchScalarGridSpec(
        num_scalar_prefetch=1,
        grid=(b,),
        in_specs=[pl.BlockSpec((1, D_HEADS, D_KV_LORA), lambda i, pt: (i, 0, 0)),
                  pl.BlockSpec((1, D_HEADS, D_ROPE), lambda i, pt: (i, 0, 0)),
                  pl.BlockSpec((1, 1, D_KV_LORA), lambda i, pt: (i, 0, 0)),
                  pl.BlockSpec((1, 1, D_ROPE), lambda i, pt: (i, 0, 0)),
                  any_spec, any_spec],
        out_specs=pl.BlockSpec((1, D_HEADS, D_KV_LORA), lambda i, pt: (i, 0, 0)),
        scratch_shapes=[pltpu.VMEM((2, past, D_KV_LORA), F32),
                        pltpu.VMEM((2, past, D_ROPE), F32),
                        pltpu.SemaphoreType.DMA((2, 2))])
    return pl.pallas_call(
        functools.partial(_mla_decode_kernel, layer=layer, n_pages=n_pages),
        grid_spec=grid_spec,
        out_shape=jax.ShapeDtypeStruct((b, D_HEADS, D_KV_LORA), BF16),
        compiler_params=_cparams(("arbitrary",)),
        name="mla_decode",
    )(page_table, qa, qp, ckv_new, kpe_new, ckv_pool, kpe_pool)


def _rmsnorm(x, g, eps=EPS):
    return x * lax.rsqrt(jnp.mean(x * x, axis=-1, keepdims=True) + eps) * g


def _rope(x, pos):
    half = x.shape[-1] // 2
    inv = ROPE_THETA ** (-jnp.arange(half, dtype=F32) / half)
    ang = pos.astype(F32)[:, None] * inv[None, :]
    shape = (x.shape[0],) + (1,) * (x.ndim - 2) + (half,)
    cos = jnp.cos(ang).reshape(shape)
    sin = jnp.sin(ang).reshape(shape)
    x1, x2 = x[..., :half], x[..., half:]
    return jnp.concatenate([x1 * cos - x2 * sin, x2 * cos + x1 * sin], axis=-1)


def _layer(h, layer, wts, small, cache, page_table, state, dims):
    bp, tp, bs = dims
    p_rows = bp * tp
    m = h.shape[0]
    prm = {name: arr[layer] for name, arr in small.items()}
    pos = jnp.concatenate([jnp.tile(jnp.arange(tp, dtype=I32), bp),
                           jnp.full((bs,), page_table.shape[1] * PAGE_SIZE, I32)])

    xn, z = norm_matmul(h, wts['norm_mix_g'], wts['w_mix'], layer, tn=640, emit_xn=True)
    o1 = A_IN
    o2 = o1 + B_IN
    o3 = o2 + C_PAD
    za, zb, zc, zd = z[:, :o1], z[:, o1:o2], z[:, o2:o2 + C_IN], z[:, o3:o3 + D_IN]

    za_p = za[:p_rows].reshape(bp, tp, A_IN)
    prev_p = jnp.concatenate([jnp.zeros((bp, 1, A_IN), F32), za_p[:, :-1]], axis=1).reshape(p_rows, A_IN)
    z_prev = jnp.concatenate([prev_p, state['shift']], axis=0)
    zs = za + (z_prev - za) * prm['a_mu']
    r, k, v = zs[:, :BR], zs[:, BR:2 * BR], zs[:, 2 * BR:3 * BR]
    xw, xa = zs[:, 3 * BR:3 * BR + A_DECAY_LORA], zs[:, 3 * BR + A_DECAY_LORA:]
    lora = matmul(jnp.concatenate([jnp.tanh(xw), xa], axis=1).astype(BF16), wts['a_lora'], layer, tn=2 * BR)
    decay = jnp.exp(-DECAY_SCALE * jax.nn.sigmoid(prm['a_w0'] + lora[:, :BR]))
    a = jax.nn.sigmoid(prm['a_a0'] + lora[:, BR:])
    hd = lambda x: x.reshape(m, A_HEADS, A_HD)
    kk = hd(k * prm['a_kk'])
    kk = kk / jnp.maximum(jnp.sqrt(jnp.sum(kk * kk, axis=-1, keepdims=True)), 1e-12)
    k2 = hd(k * (1.0 + (a - 1.0) * prm['a_ka']))
    r3, v3, w3, a3 = hd(r), hd(v), hd(decay), hd(a)
    bonus = jnp.sum(r3 * k2 * prm['a_rk'], axis=-1, keepdims=True) * v3

    def scan_group(sl, b, t, s0):
        tr = lambda x: jnp.transpose(x[sl].reshape(b, t, A_HEADS, A_HD), (0, 2, 1, 3))
        y, s_t = rwkv_scan(tr(r3), tr(w3), tr(k2), tr(v3), tr(-kk), tr(kk * a3), s0)
        return jnp.transpose(y, (0, 2, 1, 3)).reshape(b * t, A_HEADS, A_HD), s_t

    y_p, s_p = scan_group(slice(0, p_rows), bp, tp, jnp.zeros((bp, A_HEADS, A_HD, A_HD), F32))
    y_s, s_s = scan_group(slice(p_rows, m), bs, 1, state['rwkv'])
    y = jnp.concatenate([y_p, y_s], axis=0)
    mu = jnp.mean(y, axis=-1, keepdims=True)
    var = jnp.mean(jnp.square(y - mu), axis=-1, keepdims=True)
    y = ((y - mu) * lax.rsqrt(var + A_LN_EPS) * prm['a_lnx_g'].reshape(A_HEADS, A_HD)
         + prm['a_lnx_b'].reshape(A_HEADS, A_HD))
    o_a = (y + bonus).reshape(m, BR)
    shift_p = za_p[:, -1]
    shift_s = za[p_rows:]

    ob_p = chunk_mlp(zb[:p_rows].reshape(bp, tp, B_IN), prm['b_ln_g'], prm['b_ln_b'],
                     prm['b_ws'], prm['b_bs']).reshape(p_rows, BR)
    zb_s = zb[p_rows:]
    xs = zb_s[:, BR:]
    mu = jnp.mean(xs, axis=-1, keepdims=True)
    var = jnp.mean(jnp.square(xs - mu), axis=-1, keepdims=True)
    v_rows = (xs - mu) * lax.rsqrt(var + EPS) * prm['b_ln_g'] + prm['b_ln_b']
    w00 = jnp.repeat(prm['b_ws'][:, 0, 0], B_GC)
    b00 = jnp.repeat(prm['b_bs'][:, 0], B_GC)
    ob_s = zb_s[:, :BR] * (v_rows.astype(BF16).astype(F32) * w00.astype(BF16).astype(F32) + b00)
    o_b = jnp.concatenate([ob_p, ob_s], axis=0)

    c1 = C_HEADS * C_HD
    c2 = c1 + C_KV_HEADS * C_HD
    c3 = c2 + C_KV_HEADS * C_HD
    c4 = c3 + C_IDX_HEADS * C_IDX_HD
    c5 = c4 + C_IDX_HD
    q = _rope(_rmsnorm(zc[:, :c1].reshape(m, C_HEADS, C_HD), prm['c_qn_g']), pos)
    kc = _rope(_rmsnorm(zc[:, c1:c2].reshape(m, C_KV_HEADS, C_HD), prm['c_kn_g']), pos)
    vc = zc[:, c2:c3].reshape(m, C_KV_HEADS, C_HD)
    qi = _rope(zc[:, c3:c4].reshape(m, C_IDX_HEADS, C_IDX_HD), pos)
    ki = _rope(zc[:, c4:c5], pos)
    wi = zc[:, c5:] * (C_IDX_HEADS ** -0.5)
    q_b, qi_b = q.astype(BF16), qi.astype(BF16)
    kvp = lambda x: jnp.transpose(x[:p_rows].reshape(bp, tp, C_KV_HEADS, C_HD), (0, 2, 1, 3)).astype(BF16)
    oc_p = dsa_prompt(jnp.transpose(q_b[:p_rows], (1, 0, 2)), jnp.transpose(qi_b[:p_rows], (1, 0, 2)),
                      wi[:p_rows], ki[:p_rows].reshape(bp, tp, C_IDX_HD).astype(BF16), kvp(kc), kvp(vc))
    oc_p = jnp.transpose(oc_p, (1, 0, 2)).reshape(p_rows, BR)
    kvw = C_KV_HEADS * C_HD
    past = page_table.shape[1] * PAGE_SIZE
    keys = dsa_decode_keys(page_table, qi_b[p_rows:], wi[p_rows:, :, None], ki[p_rows:, None, :],
                           cache['kidx'], layer)
    thr, cut = dsa_select(keys, min(C_TOPK, (past + 1) // 4))
    key_rows = jnp.transpose(keys, (1, 0, 2)).reshape(bs, 1, -1)
    oc_s = dsa_decode_attn(page_table, q_b[p_rows:], key_rows, thr[:, :, None], cut[:, :, None],
                           kc[p_rows:].reshape(bs, 1, kvw), vc[p_rows:].reshape(bs, 1, kvw),
                           cache['k'], cache['v'], layer)
    o_c = jnp.concatenate([oc_p, oc_s.reshape(bs, BR)], axis=0)

    d1 = D_Q_LORA
    d2 = d1 + D_KV_LORA
    cq = _rmsnorm(zd[:, :d1], prm['d_qa_norm_g'])
    qd = matmul(cq.astype(BF16), wts['d_w_uq'], layer, tn=D_HEADS * (D_NOPE + D_ROPE))
    qd = _rmsnorm(qd.reshape(m, D_HEADS, D_NOPE + D_ROPE), prm['d_qn_g'])
    q_nope = qd[..., :D_NOPE].reshape(m, D_HEADS * D_NOPE)
    q_pe = jnp.transpose(_rope(qd[..., D_NOPE:], pos), (1, 0, 2)).astype(BF16)
    ckv = _rmsnorm(zd[:, d1:d2], prm['d_kva_norm_g'])
    kpe = _rope(_rmsnorm(zd[:, d2:], prm['d_kn_g']), pos)
    q_abs = headmm_to_headmajor(q_nope.astype(BF16), wts['d_w_uk'], layer, out_dtype=BF16)
    ol_p = mla_prompt(q_abs[:, :p_rows], q_pe[:, :p_rows],
                      ckv[:p_rows].reshape(bp, tp, D_KV_LORA).astype(BF16),
                      kpe[:p_rows].reshape(bp, tp, D_ROPE).astype(BF16))
    ol_s = mla_decode(page_table, jnp.transpose(q_abs[:, p_rows:], (1, 0, 2)),
                      jnp.transpose(q_pe[:, p_rows:], (1, 0, 2)),
                      ckv[p_rows:, None, :], kpe[p_rows:, None, :], cache['ckv'], cache['kpe'], layer)
    o_lat = jnp.concatenate([ol_p, jnp.transpose(ol_s, (1, 0, 2))], axis=1)
    o_d = headmm_from_headmajor(o_lat, wts['d_w_uv'], layer, out_dtype=BF16)

    br = jnp.stack([o_a.astype(BF16), o_b.astype(BF16), o_c.astype(BF16), o_d], axis=0)
    merged = gate_merge(xn, br, wts['w_gate'], wts['w_branch'], layer)
    h = matmul_residual(merged, wts['w_out'], layer, h)

    up = norm_matmul(h, wts['norm_ffn_g'], wts['f_up'], layer, tn=512, emit_xn=False)
    act = conv_swiglu(up, jnp.transpose(state['conv'], (1, 0, 2)), wts['f_cw'], wts['f_cb'], layer,
                      t_prompt=tp, n_sample=bs)
    conv_p = up[:p_rows].reshape(bp, tp, -1)[:, -2:]
    conv_s = jnp.stack([state['conv'][:, 1], up[p_rows:]], axis=1)
    h = matmul_residual(act, wts['f_down'], layer, h)

    h = ple_update(h, wts['norm_ple_g'], wts['w_ple_gate'], wts['pe'], wts['w_ple_proj'], layer)

    st_p = (s_p, shift_p, conv_p, kc[:p_rows].reshape(bp, tp, C_KV_HEADS, C_HD),
            vc[:p_rows].reshape(bp, tp, C_KV_HEADS, C_HD), ki[:p_rows].reshape(bp, tp, C_IDX_HD),
            ckv[:p_rows].reshape(bp, tp, D_KV_LORA), kpe[:p_rows].reshape(bp, tp, D_ROPE))
    st_s = (s_s, shift_s, conv_s, kc[p_rows:].reshape(bs, 1, C_KV_HEADS, C_HD),
            vc[p_rows:].reshape(bs, 1, C_KV_HEADS, C_HD), ki[p_rows:].reshape(bs, 1, C_IDX_HD),
            ckv[p_rows:].reshape(bs, 1, D_KV_LORA), kpe[p_rows:].reshape(bs, 1, D_ROPE),
            v_rows.reshape(bs, 1, BR))
    return h, st_p, st_s


def kernel(x_prompt, x_sample, state_rwkv, state_rwkv_shift, state_ffn_conv, cache_dsa_k, cache_dsa_v, cache_dsa_kidx, cache_mla_ckv, cache_mla_kpe, page_table, p_prompt, p_sample, norm_mix_g, w_in, a_mu, a_w0, a_w2, a_a0, a_a2, a_kk, a_ka, a_rk, a_lnx_g, a_lnx_b, b_ln_g, b_ln_b, b_ws, b_bs, c_qn_g, c_kn_g, d_qa_norm_g, d_w_uq, d_qn_g, d_kva_norm_g, d_kn_g, d_w_uk, d_w_uv, w_branch, w_out, norm_ffn_g, f_up, f_cw, f_cb, f_down, norm_ple_g, w_ple_gate, w_ple_proj):
    bp, tp, _ = x_prompt.shape
    bs = x_sample.shape[0]
    depth = w_in.shape[0]
    p_rows = bp * tp
    h = jnp.concatenate([x_prompt.reshape(p_rows, D_MODEL), x_sample.reshape(bs, D_MODEL)], axis=0)
    o2 = A_IN + B_IN
    o3 = o2 + C_IN
    o4 = o3 + D_IN
    w_in_b = w_in.astype(BF16)
    pad = lambda x, n: jnp.pad(x, ((0, 0), (0, 0), (0, n - x.shape[2])))
    zeros = jnp.zeros((depth, A_DECAY_LORA, BR), F32)
    gain = lambda g: g.reshape(depth, 1, D_MODEL)
    wts = dict(
        w_mix=jnp.concatenate([w_in_b[:, :, :o2], pad(w_in_b[:, :, o2:o3], C_PAD), pad(w_in_b[:, :, o3:o4], D_PAD)], axis=2),
        w_gate=w_in_b[:, :, o4:],
        a_lora=jnp.concatenate([jnp.concatenate([a_w2, zeros], axis=2),
                                jnp.concatenate([zeros, a_a2], axis=2)], axis=1).astype(BF16),
        d_w_uq=d_w_uq.astype(BF16),
        d_w_uk=jnp.transpose(d_w_uk, (0, 2, 3, 1)).astype(BF16),
        d_w_uv=jnp.transpose(d_w_uv, (0, 2, 1, 3)).astype(BF16),
        w_branch=w_branch.astype(BF16), w_out=w_out.astype(BF16),
        f_up=f_up.astype(BF16), f_down=f_down.astype(BF16),
        w_ple_gate=w_ple_gate.astype(BF16), w_ple_proj=w_ple_proj.astype(BF16),
        norm_mix_g=gain(norm_mix_g), norm_ffn_g=gain(norm_ffn_g), norm_ple_g=gain(norm_ple_g),
        pe=jnp.concatenate([p_prompt.reshape(depth, p_rows, -1), p_sample.reshape(depth, bs, -1)], axis=1).astype(BF16))
    small = dict(a_mu=a_mu, a_w0=a_w0, a_a0=a_a0, a_kk=a_kk, a_ka=a_ka, a_rk=a_rk, a_lnx_g=a_lnx_g,
                 a_lnx_b=a_lnx_b, b_ln_g=b_ln_g, b_ln_b=b_ln_b, b_ws=b_ws, b_bs=b_bs, c_qn_g=c_qn_g,
                 c_kn_g=c_kn_g, d_qa_norm_g=d_qa_norm_g, d_qn_g=d_qn_g, d_kva_norm_g=d_kva_norm_g,
                 d_kn_g=d_kn_g)
    wts.update(f_cw=f_cw, f_cb=f_cb.reshape(depth, 1, -1))
    kvw = C_KV_HEADS * C_HD
    cache = dict(k=cache_dsa_k.reshape(cache_dsa_k.shape[:3] + (kvw,)),
                 v=cache_dsa_v.reshape(cache_dsa_v.shape[:3] + (kvw,)),
                 kidx=cache_dsa_kidx, ckv=cache_mla_ckv, kpe=cache_mla_kpe)
    sp, ss = [], []
    for i in range(depth):
        state = dict(rwkv=state_rwkv[i], shift=state_rwkv_shift[i], conv=state_ffn_conv[i])
        h, st_p, st_s = _layer(h, i, wts, small, cache, page_table, state, (bp, tp, bs))
        sp.append(st_p)
        ss.append(st_s)

    def stk(lst, j):
        return jnp.stack([s[j] for s in lst], axis=0)

    return ((h[:p_rows].reshape(bp, tp, D_MODEL), h[p_rows:].reshape(bs, 1, D_MODEL))
            + tuple(stk(sp, j) for j in range(8)) + tuple(stk(ss, j) for j in range(9)))
```

```python
import functools

import jax
import jax.numpy as jnp
from jax import lax
from jax.experimental import pallas as pl
from jax.experimental.pallas import tpu as pltpu

F32 = jnp.float32
BF16 = jnp.bfloat16
I32 = jnp.int32

D_MODEL = 2048
N_BRANCH = 4
BR = D_MODEL // N_BRANCH
EPS = 1e-6
ROPE_THETA = 10000.0
Q_BLOCK = 128
A_HD = 64
A_HEADS = BR // A_HD
A_DECAY_LORA = 64
A_AAA_LORA = 64
A_IN = 3 * BR + A_DECAY_LORA + A_AAA_LORA
A_LN_EPS = 64e-5
DECAY_SCALE = 0.606531
CHUNK = 128
B_GROUPS = 4
B_GC = BR // B_GROUPS
B_IN = 2 * BR
C_HD = 64
C_HEADS = BR // C_HD
C_KV_HEADS = 2
C_IDX_HEADS = 4
C_IDX_HD = 64
C_TOPK = 256
C_IN = C_HEADS * C_HD + 2 * C_KV_HEADS * C_HD + C_IDX_HEADS * C_IDX_HD + C_IDX_HD + C_IDX_HEADS
D_NOPE = 64
D_ROPE = 32
D_V = 64
D_HEADS = BR // D_V
D_Q_LORA = 3 * D_MODEL // 16
D_KV_LORA = D_MODEL // 16
D_IN = D_Q_LORA + D_KV_LORA + D_ROPE
PAGE_SIZE = 128

LANE = 128
C_PAD = 1152
D_PAD = 640
N_MIX = A_IN + B_IN + C_PAD + D_PAD
VMEM_LIMIT = 56 * 1024 * 1024
ROW_TILE = 1040
KEY_CHUNK = 512
DEC_SEQS = 8
NEG = -1e30
INT_MIN = -2147483648


def _pick_tile(n, target, mult=8):
    best = n
    for t in range(mult, min(n, target) + 1, mult):
        if n % t == 0:
            best = t
    return best


def _cparams(sem):
    return pltpu.CompilerParams(dimension_semantics=sem, vmem_limit_bytes=VMEM_LIMIT)


def _rowmax(x):
    part = x[:, 0:LANE]
    for u in range(1, x.shape[1] // LANE):
        part = jnp.maximum(part, x[:, u * LANE:(u + 1) * LANE])
    return jnp.max(part, axis=1, keepdims=True)


def _rowsum(x):
    part = x[:, 0:LANE]
    for u in range(1, x.shape[1] // LANE):
        part = part + x[:, u * LANE:(u + 1) * LANE]
    return jnp.sum(part, axis=1, keepdims=True)


def _dot(a, b):
    return jnp.dot(a, b, preferred_element_type=F32)


def _dot_nt(a, b):
    return lax.dot_general(a, b, (((1,), (1,)), ((), ())), preferred_element_type=F32)


def _rms_rows(h_ref, g_ref):
    x = h_ref[...]
    ms = jnp.mean(x * x, axis=-1, keepdims=True)
    return (x * lax.rsqrt(ms + EPS) * g_ref[...]).astype(BF16)


def _norm_mm_kernel(h_ref, g_ref, w_ref, *rest, emit_xn):
    if emit_xn:
        xn_ref, z_ref, xs_ref = rest
    else:
        z_ref, xs_ref = rest

    @pl.when(pl.program_id(1) == 0)
    def _():
        xn = _rms_rows(h_ref, g_ref)
        xs_ref[...] = xn
        if emit_xn:
            xn_ref[...] = xn

    z_ref[...] = _dot(xs_ref[...], w_ref[...])


def norm_matmul(h, g, w, layer, *, tn, emit_xn, tm_target=ROW_TILE):
    m, k = h.shape
    n = w.shape[2]
    tm = _pick_tile(m, tm_target)
    out_shape = [jax.ShapeDtypeStruct((m, n), F32)]
    out_specs = [pl.BlockSpec((tm, tn), lambda i, j: (i, j))]
    if emit_xn:
        out_shape.insert(0, jax.ShapeDtypeStruct((m, k), BF16))
        out_specs.insert(0, pl.BlockSpec((tm, k), lambda i, j: (i, 0)))
    outs = pl.pallas_call(
        functools.partial(_norm_mm_kernel, emit_xn=emit_xn),
        grid=(m // tm, n // tn),
        in_specs=[pl.BlockSpec((tm, k), lambda i, j: (i, 0)),
                  pl.BlockSpec((None, 1, k), lambda i, j: (layer, 0, 0)),
                  pl.BlockSpec((None, k, tn), lambda i, j: (layer, 0, j))],
        out_specs=out_specs,
        out_shape=out_shape,
        scratch_shapes=[pltpu.VMEM((tm, k), BF16)],
        compiler_params=_cparams(("parallel", "arbitrary")),
        name="norm_matmul",
    )(h, g, w)
    return outs if emit_xn else outs[0]


def _gate_merge_kernel(xn_ref, br_ref, wg_ref, wb_ref, o_ref, acc_ref):
    kk = pl.program_id(2)
    gate = jax.nn.sigmoid(_dot(xn_ref[...], wg_ref[...]))
    up = _dot(br_ref[0], wb_ref[...])
    contrib = gate * up

    @pl.when(kk == 0)
    def _():
        acc_ref[...] = contrib

    @pl.when(kk > 0)
    def _():
        acc_ref[...] += contrib

    @pl.when(kk == N_BRANCH - 1)
    def _():
        o_ref[...] = acc_ref[...].astype(BF16)


def gate_merge(xn, br, wg, wb, layer, *, tn=512, tm_target=ROW_TILE):
    m, k = xn.shape
    tm = _pick_tile(m, tm_target)
    nj = D_MODEL // tn
    return pl.pallas_call(
        _gate_merge_kernel,
        grid=(m // tm, nj, N_BRANCH),
        in_specs=[pl.BlockSpec((tm, k), lambda i, j, c: (i, 0)),
                  pl.BlockSpec((1, tm, BR), lambda i, j, c: (c, i, 0)),
                  pl.BlockSpec((None, k, tn), lambda i, j, c: (layer, 0, c * nj + j)),
                  pl.BlockSpec((None, None, BR, tn), lambda i, j, c: (layer, c, 0, j))],
        out_specs=pl.BlockSpec((tm, tn), lambda i, j, c: (i, j)),
        out_shape=jax.ShapeDtypeStruct((m, D_MODEL), BF16),
        scratch_shapes=[pltpu.VMEM((tm, tn), F32)],
        compiler_params=_cparams(("parallel", "parallel", "arbitrary")),
        name="gate_merge",
    )(xn, br, wg, wb)


def _mm_res_kernel(x_ref, w_ref, r_ref, o_ref):
    o_ref[...] = r_ref[...] + _dot(x_ref[...], w_ref[...])


def matmul_residual(x, w, layer, res, *, tn=512, tm_target=ROW_TILE):
    m, k = x.shape
    n = w.shape[2]
    tm = _pick_tile(m, tm_target)
    return pl.pallas_call(
        _mm_res_kernel,
        grid=(m // tm, n // tn),
        in_specs=[pl.BlockSpec((tm, k), lambda i, j: (i, 0)),
                  pl.BlockSpec((None, k, tn), lambda i, j: (layer, 0, j)),
                  pl.BlockSpec((tm, tn), lambda i, j: (i, j))],
        out_specs=pl.BlockSpec((tm, tn), lambda i, j: (i, j)),
        out_shape=jax.ShapeDtypeStruct((m, n), F32),
        compiler_params=_cparams(("parallel", "parallel")),
        name="matmul_residual",
    )(x, w, res)


def _mm_kernel(x_ref, w_ref, o_ref):
    o_ref[...] = _dot(x_ref[...], w_ref[...]).astype(o_ref.dtype)


def matmul(x, w, layer, *, tn, out_dtype=F32, tm_target=ROW_TILE):
    m, k = x.shape
    n = w.shape[2]
    tm = _pick_tile(m, tm_target)
    return pl.pallas_call(
        _mm_kernel,
        grid=(m // tm, n // tn),
        in_specs=[pl.BlockSpec((tm, k), lambda i, j: (i, 0)),
                  pl.BlockSpec((None, k, tn), lambda i, j: (layer, 0, j))],
        out_specs=pl.BlockSpec((tm, tn), lambda i, j: (i, j)),
        out_shape=jax.ShapeDtypeStruct((m, n), out_dtype),
        compiler_params=_cparams(("parallel", "parallel")),
        name="matmul",
    )(x, w)


def _ple_kernel(h_ref, g_ref, wg_ref, pe_ref, wp_ref, r_ref, o_ref, xs_ref):
    @pl.when(pl.program_id(1) == 0)
    def _():
        xs_ref[...] = _rms_rows(h_ref, g_ref)

    gate = jax.nn.sigmoid(_dot(xs_ref[...], wg_ref[...]))
    o_ref[...] = r_ref[...] + gate * _dot(pe_ref[...], wp_ref[...])


def ple_update(h, g, wg, pe, wp, layer, *, tn=512, tm_target=ROW_TILE):
    m, k = h.shape
    kp = pe.shape[2]
    tm = _pick_tile(m, tm_target)
    return pl.pallas_call(
        _ple_kernel,
        grid=(m // tm, D_MODEL // tn),
        in_specs=[pl.BlockSpec((tm, k), lambda i, j: (i, 0)),
                  pl.BlockSpec((None, 1, k), lambda i, j: (layer, 0, 0)),
                  pl.BlockSpec((None, k, tn), lambda i, j: (layer, 0, j)),
                  pl.BlockSpec((None, tm, kp), lambda i, j: (layer, i, 0)),
                  pl.BlockSpec((None, kp, tn), lambda i, j: (layer, 0, j)),
                  pl.BlockSpec((tm, tn), lambda i, j: (i, j))],
        out_specs=pl.BlockSpec((tm, tn), lambda i, j: (i, j)),
        out_shape=jax.ShapeDtypeStruct((m, D_MODEL), F32),
        scratch_shapes=[pltpu.VMEM((tm, k), BF16)],
        compiler_params=_cparams(("parallel", "arbitrary")),
        name="ple_update",
    )(h, g, wg, pe, wp, h)


def _headmm_out_hm_kernel(x_ref, w_ref, o_ref, *, heads, din):
    for hh in range(heads):
        o_ref[hh] = _dot(x_ref[:, hh * din:(hh + 1) * din], w_ref[hh]).astype(o_ref.dtype)


def headmm_to_headmajor(x, w, layer, *, out_dtype, tm_target=ROW_TILE):
    m = x.shape[0]
    _, heads, din, dout = w.shape
    tm = _pick_tile(m, tm_target)
    return pl.pallas_call(
        functools.partial(_headmm_out_hm_kernel, heads=heads, din=din),
        grid=(m // tm,),
        in_specs=[pl.BlockSpec((tm, heads * din), lambda i: (i, 0)),
                  pl.BlockSpec((None, heads, din, dout), lambda i: (layer, 0, 0, 0))],
        out_specs=pl.BlockSpec((heads, tm, dout), lambda i: (0, i, 0)),
        out_shape=jax.ShapeDtypeStruct((heads, m, dout), out_dtype),
        compiler_params=_cparams(("parallel",)),
        name="headmm_to_headmajor",
    )(x, w)


def _headmm_in_hm_kernel(x_ref, w_ref, o_ref, *, heads, dout):
    for hh in range(heads):
        o_ref[:, hh * dout:(hh + 1) * dout] = _dot(x_ref[hh], w_ref[hh]).astype(o_ref.dtype)


def headmm_from_headmajor(x, w, layer, *, out_dtype, tm_target=ROW_TILE):
    heads, m, din = x.shape
    dout = w.shape[3]
    tm = _pick_tile(m, tm_target)
    return pl.pallas_call(
        functools.partial(_headmm_in_hm_kernel, heads=heads, dout=dout),
        grid=(m // tm,),
        in_specs=[pl.BlockSpec((heads, tm, din), lambda i: (0, i, 0)),
                  pl.BlockSpec((None, heads, din, dout), lambda i: (layer, 0, 0, 0))],
        out_specs=pl.BlockSpec((tm, heads * dout), lambda i: (i, 0)),
        out_shape=jax.ShapeDtypeStruct((m, heads * dout), out_dtype),
        compiler_params=_cparams(("parallel",)),
        name="headmm_from_headmajor",
    )(x, w)


def _conv_swiglu_kernel(ua_ref, ub_ref, ha_ref, hb_ref, sa_ref, sb_ref, cwa_ref, cwb_ref, cba_ref, cbb_ref,
                        o_ref, *, tm, t_prompt, n_prompt_blocks):
    i = pl.program_id(0)
    is_sample = i >= n_prompt_blocks
    seq_start = (i * tm) % t_prompt == 0
    r = lax.broadcasted_iota(I32, ua_ref.shape, 0)

    def conv(u_ref, h_ref, s_ref, cw_ref, cb_ref):
        x = u_ref[...]
        h1 = jnp.where(seq_start, 0.0, h_ref[7:8, :])
        h2 = jnp.where(seq_start, 0.0, h_ref[6:7, :])
        x1 = jnp.where(r >= 1, pltpu.roll(x, 1, axis=0), h1)
        x2 = jnp.where(r >= 2, pltpu.roll(x, 2, axis=0), jnp.where(r == 1, h1, h2))
        x1 = jnp.where(is_sample, s_ref[1], x1)
        x2 = jnp.where(is_sample, s_ref[0], x2)
        return cb_ref[...] + x2 * cw_ref[0:1, :] + x1 * cw_ref[1:2, :] + x * cw_ref[2:3, :]

    ca = conv(ua_ref, ha_ref, sa_ref, cwa_ref, cba_ref)
    cb = conv(ub_ref, hb_ref, sb_ref, cwb_ref, cbb_ref)
    o_ref[...] = (ca * jax.nn.sigmoid(ca) * cb).astype(o_ref.dtype)


def conv_swiglu(up, conv0, cw, cb, layer, *, t_prompt, n_sample, tn=2816):
    m, f2 = up.shape
    f = f2 // 2
    tm = _pick_tile(n_sample, 128)
    p_rows = m - n_sample
    assert p_rows % tm == 0 and t_prompt % tm == 0 and tm % 8 == 0
    npb = p_rows // tm
    nj = f // tn
    hb = tm // 8
    main_a = pl.BlockSpec((tm, tn), lambda i, j: (i, j))
    main_b = pl.BlockSpec((tm, tn), lambda i, j: (i, j + nj))
    halo_a = pl.BlockSpec((8, tn), lambda i, j: (jnp.maximum(i * hb - 1, 0), j))
    halo_b = pl.BlockSpec((8, tn), lambda i, j: (jnp.maximum(i * hb - 1, 0), j + nj))
    st_a = pl.BlockSpec((2, tm, tn), lambda i, j: (0, jnp.maximum(i - npb, 0), j))
    st_b = pl.BlockSpec((2, tm, tn), lambda i, j: (0, jnp.maximum(i - npb, 0), j + nj))
    cw_a = pl.BlockSpec((None, 3, tn), lambda i, j: (layer, 0, j))
    cw_b = pl.BlockSpec((None, 3, tn), lambda i, j: (layer, 0, j + nj))
    cb_a = pl.BlockSpec((None, 1, tn), lambda i, j: (layer, 0, j))
    cb_b = pl.BlockSpec((None, 1, tn), lambda i, j: (layer, 0, j + nj))
    return pl.pallas_call(
        functools.partial(_conv_swiglu_kernel, tm=tm, t_prompt=t_prompt, n_prompt_blocks=npb),
        grid=(m // tm, nj),
        in_specs=[main_a, main_b, halo_a, halo_b, st_a, st_b, cw_a, cw_b, cb_a, cb_b],
        out_specs=pl.BlockSpec((tm, tn), lambda i, j: (i, j)),
        out_shape=jax.ShapeDtypeStruct((m, f), BF16),
        compiler_params=_cparams(("parallel", "parallel")),
        name="conv_swiglu",
    )(up, up, up, up, conv0, conv0, cw, cw, cb, cb)


def _rwkv_scan_kernel(r_ref, w_ref, k_ref, v_ref, nkk_ref, bb_ref, s0_ref, y_ref, sT_ref, s_ref, *, tc, nb):
    c = pl.program_id(1)

    @pl.when(c == 0)
    def _():
        s_ref[...] = s0_ref[...]

    row = lax.broadcasted_iota(I32, (A_HD, A_HD), 0)
    col = lax.broadcasted_iota(I32, (A_HD, A_HD), 1)
    eye = (row == col).astype(F32)

    def step(t, carry):
        sl = (slice(None), slice(None), pl.ds(t, 1), slice(None))
        s_prev = s_ref[...]
        sa = jnp.sum(s_prev * nkk_ref[sl], axis=-1, keepdims=True)
        v_col = jnp.sum(eye * v_ref[sl], axis=-1, keepdims=True)
        s_new = s_prev * w_ref[sl] + sa * bb_ref[sl] + v_col * k_ref[sl]
        s_ref[...] = s_new
        y_col = jnp.sum(s_new * r_ref[sl], axis=-1, keepdims=True)
        y_ref[sl] = jnp.sum(eye * y_col, axis=-2, keepdims=True)
        return carry

    lax.fori_loop(0, tc, step, 0)

    @pl.when(c == pl.num_programs(1) - 1)
    def _():
        sT_ref[...] = s_ref[...]


def rwkv_scan(r, w, k, v, nkk, bb, s0):
    b, hds, t, n = r.shape
    tc = _pick_tile(t, 256)
    nb = 2 if b % 2 == 0 else 1
    vec = pl.BlockSpec((nb, hds, tc, n), lambda i, c: (i, 0, c, 0))
    st = pl.BlockSpec((nb, hds, n, n), lambda i, c: (i, 0, 0, 0))
    return pl.pallas_call(
        functools.partial(_rwkv_scan_kernel, tc=tc, nb=nb),
        grid=(b // nb, t // tc),
        in_specs=[vec, vec, vec, vec, vec, vec, st],
        out_specs=[vec, st],
        out_shape=[jax.ShapeDtypeStruct((b, hds, t, n), F32), jax.ShapeDtypeStruct((b, hds, n, n), F32)],
        scratch_shapes=[pltpu.VMEM((nb, hds, n, n), F32)],
        compiler_params=_cparams(("parallel", "arbitrary")),
        name="rwkv_scan",
    )(r, w, k, v, nkk, bb, s0)


def _chunk_mlp_kernel(z_ref, lg_ref, lb_ref, ws_ref, bst_ref, o_ref):
    u = z_ref[0, :, :BR]
    x = z_ref[0, :, BR:]
    mu = jnp.mean(x, axis=-1, keepdims=True)
    var = jnp.mean(jnp.square(x - mu), axis=-1, keepdims=True)
    vn = (x - mu) * lax.rsqrt(var + EPS) * lg_ref[...] + lb_ref[...]
    tcn = ws_ref.shape[1]
    causal = lax.broadcasted_iota(I32, (tcn, tcn), 0) >= lax.broadcasted_iota(I32, (tcn, tcn), 1)
    for g in range(B_GROUPS):
        wmat = jnp.where(causal, ws_ref[g], 0.0).astype(BF16)
        s = _dot(wmat, vn[:, g * B_GC:(g + 1) * B_GC].astype(BF16)) + bst_ref[:, g:g + 1]
        o_ref[0, :, g * B_GC:(g + 1) * B_GC] = u[:, g * B_GC:(g + 1) * B_GC] * s


def chunk_mlp(zb, ln_g, ln_b, ws, bs):
    b, t, _ = zb.shape
    tc = min(t, CHUNK)
    return pl.pallas_call(
        _chunk_mlp_kernel,
        grid=(b, t // tc),
        in_specs=[pl.BlockSpec((1, tc, B_IN), lambda i, c: (i, c, 0)),
                  pl.BlockSpec((1, BR), lambda i, c: (0, 0)),
                  pl.BlockSpec((1, BR), lambda i, c: (0, 0)),
                  pl.BlockSpec((B_GROUPS, tc, tc), lambda i, c: (0, 0, 0)),
                  pl.BlockSpec((tc, B_GROUPS), lambda i, c: (0, 0))],
        out_specs=pl.BlockSpec((1, tc, BR), lambda i, c: (i, c, 0)),
        out_shape=jax.ShapeDtypeStruct((b, t, BR), F32),
        compiler_params=_cparams(("parallel", "parallel")),
        name="chunk_mlp",
    )(zb, ln_g.reshape(1, BR), ln_b.reshape(1, BR), ws[:, :tc, :tc], bs[:, :tc].T)


def _sortable_key(score):
    score = jnp.where(score == 0.0, 0.0, score)
    bits = pltpu.bitcast(score, I32)
    return jnp.where(bits < 0, bits ^ 0x7FFFFFFF, bits)


def _count_rows(key_ref, n_chunks, rows, wk, pred):
    def body(c, acc):
        hit = pred(c, key_ref[c]).astype(I32)
        part = hit[:, 0:LANE]
        for u in range(1, wk // LANE):
            part = part + hit[:, u * LANE:(u + 1) * LANE]
        return acc + part

    acc = lax.fori_loop(0, n_chunks, body, jnp.zeros((rows, LANE), I32))
    return jnp.sum(acc, axis=1, keepdims=True)


def _select_threshold(key_ref, n_chunks, rows, wk, n_sel, n_pos):
    s_off = lax.broadcasted_iota(I32, (rows, wk), 1)

    def bit_body(i, cur):
        bit = lax.shift_left(jnp.int32(1), jnp.int32(31) - i)
        trial = cur | bit
        th = trial ^ INT_MIN
        ok = _count_rows(key_ref, n_chunks, rows, wk, lambda c, kk: kk >= th) >= n_sel
        return jnp.where(ok, trial, cur)

    thr = lax.fori_loop(0, 32, bit_body, jnp.zeros((rows, 1), I32)) ^ INT_MIN
    n_ge = _count_rows(key_ref, n_chunks, rows, wk, lambda c, kk: kk >= thr)
    n_gt = _count_rows(key_ref, n_chunks, rows, wk, lambda c, kk: kk > thr)
    need = n_sel - n_gt
    tie_rows = jnp.where((n_ge > n_sel) & (thr != INT_MIN), 1, 0)
    any_tie = jnp.max(tie_rows)

    def tie_search():
        nbits = max(1, (n_pos - 1).bit_length())

        def body(i, lohi):
            lo, hi = lohi
            mid = lax.shift_right_arithmetic(lo + hi, 1)
            cnt = _count_rows(key_ref, n_chunks, rows, wk,
                              lambda c, kk: (kk == thr) & (c * wk + s_off <= mid))
            ok = cnt >= need
            return jnp.where(ok, lo, mid), jnp.where(ok, mid, hi)

        lo = jnp.full((rows, 1), -1, I32)
        hi = jnp.full((rows, 1), n_pos - 1, I32)
        return lax.fori_loop(0, nbits + 1, body, (lo, hi))[1]

    cut = lax.cond(any_tie > 0, tie_search, lambda: jnp.full((rows, 1), n_pos - 1, I32))
    return thr, cut


def _dsa_prompt_kernel(q_ref, qi_ref, wi_ref, ki_ref, k_ref, v_ref, o_ref,
                       key_ref, m_ref, l_ref, acc_ref, *, t_total, wk, n_sel):
    qb = pl.program_id(1)
    tq = Q_BLOCK
    n_chunks = (qb * tq + tq - 1) // wk + 1
    t_pos = qb * tq + lax.broadcasted_iota(I32, (tq, wk), 0)
    s_off = lax.broadcasted_iota(I32, (tq, wk), 1)

    def score_body(c, carry):
        k0 = pl.multiple_of(c * wk, wk)
        ki = ki_ref[0, pl.ds(k0, wk), :]
        sc = jnp.zeros((tq, wk), F32)
        for hh in range(C_IDX_HEADS):
            e = _dot_nt(qi_ref[hh], ki) * (C_IDX_HD ** -0.5)
            sc = sc + wi_ref[:, hh:hh + 1] * jnp.maximum(e, 0.0)
        key_ref[c] = jnp.where(k0 + s_off <= t_pos, _sortable_key(sc), INT_MIN)
        return carry

    lax.fori_loop(0, n_chunks, score_body, 0)
    thr, cut = _select_threshold(key_ref, n_chunks, tq, wk, n_sel, t_total)
    tie_ok = thr != INT_MIN

    m_ref[...] = jnp.full(m_ref.shape, NEG, F32)
    l_ref[...] = jnp.zeros(l_ref.shape, F32)
    acc_ref[...] = jnp.zeros(acc_ref.shape, F32)
    grp = C_HEADS // C_KV_HEADS
    rows = grp * tq

    def attn_body(c, carry):
        k0 = pl.multiple_of(c * wk, wk)
        kk = key_ref[c]
        sel = ((kk > thr) | ((kk == thr) & (k0 + s_off <= cut) & tie_ok))[None]
        for n in range(C_KV_HEADS):
            qn = q_ref[n * grp:(n + 1) * grp].reshape(rows, C_HD)
            kn = k_ref[0, n, pl.ds(k0, wk), :]
            vn = v_ref[0, n, pl.ds(k0, wk), :]
            s = _dot_nt(qn, kn) * (C_HD ** -0.5)
            s = jnp.where(sel, s.reshape(grp, tq, wk), NEG).reshape(rows, wk)
            rs = slice(n * rows, (n + 1) * rows)
            m_old = m_ref[rs]
            m_new = jnp.maximum(m_old, _rowmax(s))
            p = jnp.exp(s - jnp.tile(m_new, (1, wk // LANE)))
            alpha = jnp.exp(m_old - m_new)
            l_ref[rs] = alpha * l_ref[rs] + _rowsum(p)
            acc_ref[rs] = alpha[:, :C_HD] * acc_ref[rs] + _dot(p.astype(BF16), vn)
            m_ref[rs] = m_new
        return carry

    lax.fori_loop(0, n_chunks, attn_body, 0)
    o_ref[...] = (acc_ref[...] / l_ref[:, :C_HD]).reshape(C_HEADS, tq, C_HD)


def dsa_prompt(q, qi, wi, ki, k, v):
    b, t, _ = ki.shape
    tq = Q_BLOCK
    nq = t // tq
    wk = min(KEY_CHUNK, t)
    n_sel = min(C_TOPK, t // 4)
    kern = functools.partial(_dsa_prompt_kernel, t_total=t, wk=wk, n_sel=n_sel)
    return pl.pallas_call(
        kern,
        grid=(b, nq),
        in_specs=[pl.BlockSpec((C_HEADS, tq, C_HD), lambda i, j: (0, i * nq + j, 0)),
                  pl.BlockSpec((C_IDX_HEADS, tq, C_IDX_HD), lambda i, j: (0, i * nq + j, 0)),
                  pl.BlockSpec((tq, C_IDX_HEADS), lambda i, j: (i * nq + j, 0)),
                  pl.BlockSpec((1, t, C_IDX_HD), lambda i, j: (i, 0, 0)),
                  pl.BlockSpec((1, C_KV_HEADS, t, C_HD), lambda i, j: (i, 0, 0, 0)),
                  pl.BlockSpec((1, C_KV_HEADS, t, C_HD), lambda i, j: (i, 0, 0, 0))],
        out_specs=pl.BlockSpec((C_HEADS, tq, C_HD), lambda i, j: (0, i * nq + j, 0)),
        out_shape=jax.ShapeDtypeStruct((C_HEADS, b * t, C_HD), F32),
        scratch_shapes=[pltpu.VMEM((t // wk, tq, wk), I32),
                        pltpu.VMEM((C_HEADS * tq, LANE), F32),
                        pltpu.VMEM((C_HEADS * tq, LANE), F32),
                        pltpu.VMEM((C_HEADS * tq, C_HD), F32)],
        compiler_params=_cparams(("parallel", "arbitrary")),
        name="dsa_prompt",
    )(q, qi, wi, ki, k, v)


def _mla_prompt_kernel(qa_ref, qp_ref, ckv_ref, kpe_ref, o_ref, m_ref, l_ref, acc_ref, *, wk):
    qb = pl.program_id(1)
    tq = Q_BLOCK
    rows = D_HEADS * tq
    n_chunks = (qb * tq + tq - 1) // wk + 1
    qa = qa_ref[...].reshape(rows, D_KV_LORA)
    qp = qp_ref[...].reshape(rows, D_ROPE)
    t_pos = qb * tq + lax.broadcasted_iota(I32, (tq, wk), 0)
    s_off = lax.broadcasted_iota(I32, (tq, wk), 1)
    scale = (D_NOPE + D_ROPE) ** -0.5
    m_ref[...] = jnp.full(m_ref.shape, NEG, F32)
    l_ref[...] = jnp.zeros(l_ref.shape, F32)
    acc_ref[...] = jnp.zeros(acc_ref.shape, F32)

    def body(c, carry):
        k0 = pl.multiple_of(c * wk, wk)
        kc = ckv_ref[0, pl.ds(k0, wk), :]
        kp = kpe_ref[0, pl.ds(k0, wk), :]
        s = (_dot_nt(qa, kc) + _dot_nt(qp, kp)) * scale
        ok = (k0 + s_off <= t_pos)[None]
        s = jnp.where(ok, s.reshape(D_HEADS, tq, wk), NEG).reshape(rows, wk)
        m_old = m_ref[...]
        m_new = jnp.maximum(m_old, _rowmax(s))
        p = jnp.exp(s - jnp.tile(m_new, (1, wk // LANE)))
        alpha = jnp.exp(m_old - m_new)
        l_ref[...] = alpha * l_ref[...] + _rowsum(p)
        acc_ref[...] = alpha * acc_ref[...] + _dot(p.astype(BF16), kc)
        m_ref[...] = m_new
        return carry

    lax.fori_loop(0, n_chunks, body, 0)
    o_ref[...] = (acc_ref[...] / l_ref[...]).reshape(D_HEADS, tq, D_KV_LORA).astype(o_ref.dtype)


def mla_prompt(qa, qp, ckv, kpe):
    b, t, _ = ckv.shape
    tq = Q_BLOCK
    nq = t // tq
    wk = min(KEY_CHUNK, t)
    return pl.pallas_call(
        functools.partial(_mla_prompt_kernel, wk=wk),
        grid=(b, nq),
        in_specs=[pl.BlockSpec((D_HEADS, tq, D_KV_LORA), lambda i, j: (0, i * nq + j, 0)),
                  pl.BlockSpec((D_HEADS, tq, D_ROPE), lambda i, j: (0, i * nq + j, 0)),
                  pl.BlockSpec((1, t, D_KV_LORA), lambda i, j: (i, 0, 0)),
                  pl.BlockSpec((1, t, D_ROPE), lambda i, j: (i, 0, 0))],
        out_specs=pl.BlockSpec((D_HEADS, tq, D_KV_LORA), lambda i, j: (0, i * nq + j, 0)),
        out_shape=jax.ShapeDtypeStruct((D_HEADS, b * t, D_KV_LORA), BF16),
        scratch_shapes=[pltpu.VMEM((D_HEADS * tq, LANE), F32),
                        pltpu.VMEM((D_HEADS * tq, LANE), F32),
                        pltpu.VMEM((D_HEADS * tq, D_KV_LORA), F32)],
        compiler_params=_cparams(("parallel", "arbitrary")),
        name="mla_prompt",
    )(qa, qp, ckv, kpe)


def _page_copy(pool_ref, buf_ref, sem_ref, layer, page, slot, row0, which):
    if buf_ref.shape[-1] == pool_ref.shape[-1] and pool_ref.shape[-2] == PAGE_SIZE:
        window = (slot, pl.ds(row0, PAGE_SIZE))
    else:
        window = (slot,) + (slice(None),) * (len(pool_ref.shape) - 3) + (pl.ds(row0, PAGE_SIZE),)
    return pltpu.make_async_copy(pool_ref.at[layer, page], buf_ref.at[window], sem_ref.at[slot, which])


def _fetch_pages(pt_ref, pools, bufs, sem_ref, layer, seq, slot, n_pages, wait):
    def body(p, carry):
        page = 0 if wait else pt_ref[seq, p]
        for which, (pool_ref, buf_ref) in enumerate(zip(pools, bufs)):
            cp = _page_copy(pool_ref, buf_ref, sem_ref, layer, page, slot, p * PAGE_SIZE, which)
            if wait:
                cp.wait()
            else:
                cp.start()
        return carry

    lax.fori_loop(0, n_pages, body, 0)


def _paged_prologue(pt_ref, pools, bufs, sem_ref, layer, n_pages, seq, n_seq):
    slot = seq % 2
    args = (pt_ref, pools, bufs, sem_ref, layer)

    @pl.when(seq == 0)
    def _():
        _fetch_pages(*args, seq, slot, n_pages, wait=False)

    @pl.when(seq + 1 < n_seq)
    def _():
        _fetch_pages(*args, seq + 1, 1 - slot, n_pages, wait=False)

    _fetch_pages(*args, seq, slot, n_pages, wait=True)
    return slot


def _dsa_decode_keys_kernel(pt_ref, qi_ref, wi_ref, kin_ref, kidx_pool, key_ref, kidx_buf, sem_ref,
                            *, layer, n_pages, wk):
    past = n_pages * PAGE_SIZE
    n_seq = pl.num_programs(0) * DEC_SEQS
    lane = lax.broadcasted_iota(I32, (1, wk), 1)
    for s in range(DEC_SEQS):
        seq = pl.program_id(0) * DEC_SEQS + s
        slot = _paged_prologue(pt_ref, (kidx_pool,), (kidx_buf,), sem_ref, layer, n_pages, seq, n_seq)
        qi = qi_ref[s]
        wi = wi_ref[s]
        kidx_t = kidx_buf[slot].astype(BF16)
        e_past = _dot(qi, kidx_t) * (C_IDX_HD ** -0.5)
        key_past = _sortable_key(jnp.sum(wi * jnp.maximum(e_past, 0.0), axis=0, keepdims=True))
        kin = kin_ref[s].astype(BF16).astype(F32)
        e_new = jnp.sum(qi.astype(F32) * kin, axis=1, keepdims=True) * (C_IDX_HD ** -0.5)
        key_new = _sortable_key(jnp.sum(wi * jnp.maximum(e_new, 0.0), axis=0, keepdims=True))
        for c in range(past // wk):
            key_ref[c, s:s + 1, :] = key_past[:, c * wk:(c + 1) * wk]
        key_ref[past // wk, s:s + 1, :] = jnp.where(lane == 0, key_new, INT_MIN)


def dsa_decode_keys(page_table, qi, wi, ki_new, kidx_pool, layer):
    b, n_pages = page_table.shape
    past = n_pages * PAGE_SIZE
    wk = min(KEY_CHUNK, past)
    n_chunks = past // wk + 1
    grid_spec = pltpu.PrefetchScalarGridSpec(
        num_scalar_prefetch=1,
        grid=(b // DEC_SEQS,),
        in_specs=[pl.BlockSpec((DEC_SEQS, C_IDX_HEADS, C_IDX_HD), lambda i, pt: (i, 0, 0)),
                  pl.BlockSpec((DEC_SEQS, C_IDX_HEADS, 1), lambda i, pt: (i, 0, 0)),
                  pl.BlockSpec((DEC_SEQS, 1, C_IDX_HD), lambda i, pt: (i, 0, 0)),
                  pl.BlockSpec(memory_space=pl.ANY)],
        out_specs=pl.BlockSpec((n_chunks, DEC_SEQS, wk), lambda i, pt: (0, i, 0)),
        scratch_shapes=[pltpu.VMEM((2, C_IDX_HD, past), F32),
                        pltpu.SemaphoreType.DMA((2, 1))])
    return pl.pallas_call(
        functools.partial(_dsa_decode_keys_kernel, layer=layer, n_pages=n_pages, wk=wk),
        grid_spec=grid_spec,
        out_shape=jax.ShapeDtypeStruct((n_chunks, b, wk), I32),
        compiler_params=_cparams(("arbitrary",)),
        name="dsa_decode_keys",
    )(page_table, qi, wi, ki_new, kidx_pool)


def _dsa_select_kernel(key_ref, thr_ref, cut_ref, *, n_sel):
    n_chunks, rows, wk = key_ref.shape
    thr, cut = _select_threshold(key_ref, n_chunks, rows, wk, n_sel, n_chunks * wk)
    thr_ref[...] = thr
    cut_ref[...] = cut


def dsa_select(keys, n_sel):
    n_chunks, b, wk = keys.shape
    rows = _pick_tile(b, 128)
    return pl.pallas_call(
        functools.partial(_dsa_select_kernel, n_sel=n_sel),
        grid=(b // rows,),
        in_specs=[pl.BlockSpec((n_chunks, rows, wk), lambda i: (0, i, 0))],
        out_specs=[pl.BlockSpec((rows, 1), lambda i: (i, 0)), pl.BlockSpec((rows, 1), lambda i: (i, 0))],
        out_shape=[jax.ShapeDtypeStruct((b, 1), I32), jax.ShapeDtypeStruct((b, 1), I32)],
        compiler_params=_cparams(("parallel",)),
        name="dsa_select",
    )(keys)


def _dsa_decode_attn_kernel(pt_ref, q_ref, key_ref, thr_ref, cut_ref, kn_ref, vn_ref, k_pool, v_pool, o_ref,
                            k_buf, v_buf, sem_ref, *, layer, n_pages):
    past = n_pages * PAGE_SIZE
    slot = _paged_prologue(pt_ref, (k_pool, v_pool), (k_buf, v_buf), sem_ref, layer, n_pages,
                           pl.program_id(0), pl.num_programs(0))
    thr = thr_ref[0]
    cut = cut_ref[0]
    tie_ok = thr != INT_MIN
    key_past = key_ref[0, :, :past]
    key_new = key_ref[0, :, past:past + 1]
    pos = lax.broadcasted_iota(I32, (1, past), 1)
    sel_past = (key_past > thr) | ((key_past == thr) & (pos <= cut) & tie_ok)
    sel_new = (key_new > thr) | ((key_new == thr) & (cut >= past) & tie_ok)

    grp = C_HEADS // C_KV_HEADS
    for n in range(C_KV_HEADS):
        qn = q_ref[0, n * grp:(n + 1) * grp]
        lanes = slice(n * C_HD, (n + 1) * C_HD)
        kp_t = k_buf[slot, n].astype(BF16)
        vp_t = v_buf[slot, n].astype(BF16)
        s = jnp.where(sel_past, _dot(qn, kp_t) * (C_HD ** -0.5), NEG)
        k_new = kn_ref[0, :, lanes].astype(BF16).astype(F32)
        v_new = vn_ref[0, :, lanes].astype(BF16).astype(F32)
        s_new = jnp.sum(qn.astype(F32) * k_new, axis=1, keepdims=True) * (C_HD ** -0.5)
        s_new = jnp.where(sel_new, s_new, NEG)
        m = jnp.maximum(jnp.max(s, axis=1, keepdims=True), s_new)
        p = jnp.where(sel_past, jnp.exp(s - m), 0.0)
        p_new = jnp.where(sel_new, jnp.exp(s_new - m), 0.0)
        denom = jnp.sum(p, axis=1, keepdims=True) + p_new
        num = _dot_nt(p.astype(BF16), vp_t) + p_new.astype(BF16).astype(F32) * v_new
        o_ref[0, n * grp:(n + 1) * grp] = num / denom


def dsa_decode_attn(page_table, q, key_rows, thr, cut, k_new, v_new, k_pool, v_pool, layer):
    b, n_pages = page_table.shape
    past = n_pages * PAGE_SIZE
    n_pos = key_rows.shape[2]
    kvw = C_KV_HEADS * C_HD
    any_spec = pl.BlockSpec(memory_space=pl.ANY)
    grid_spec = pltpu.PrefetchScalarGridSpec(
        num_scalar_prefetch=1,
        grid=(b,),
        in_specs=[pl.BlockSpec((1, C_HEADS, C_HD), lambda i, pt: (i, 0, 0)),
                  pl.BlockSpec((1, 1, n_pos), lambda i, pt: (i, 0, 0)),
                  pl.BlockSpec((1, 1, 1), lambda i, pt: (i, 0, 0)),
                  pl.BlockSpec((1, 1, 1), lambda i, pt: (i, 0, 0)),
                  pl.BlockSpec((1, 1, kvw), lambda i, pt: (i, 0, 0)),
                  pl.BlockSpec((1, 1, kvw), lambda i, pt: (i, 0, 0)),
                  any_spec, any_spec],
        out_specs=pl.BlockSpec((1, C_HEADS, C_HD), lambda i, pt: (i, 0, 0)),
        scratch_shapes=[pltpu.VMEM((2, C_KV_HEADS, C_HD, past), F32),
                        pltpu.VMEM((2, C_KV_HEADS, C_HD, past), F32),
                        pltpu.SemaphoreType.DMA((2, 2))])
    return pl.pallas_call(
        functools.partial(_dsa_decode_attn_kernel, layer=layer, n_pages=n_pages),
        grid_spec=grid_spec,
        out_shape=jax.ShapeDtypeStruct((b, C_HEADS, C_HD), F32),
        compiler_params=_cparams(("arbitrary",)),
        name="dsa_decode_attn",
    )(page_table, q, key_rows, thr, cut, k_new, v_new, k_pool, v_pool)


def _mla_decode_kernel(pt_ref, qa_ref, qp_ref, cn_ref, pn_ref, ckv_pool, kpe_pool, o_ref,
                       ckv_buf, kpe_buf, sem_ref, *, layer, n_pages):
    slot = _paged_prologue(pt_ref, (ckv_pool, kpe_pool), (ckv_buf, kpe_buf), sem_ref, layer, n_pages,
                           pl.program_id(0), pl.num_programs(0))
    scale = (D_NOPE + D_ROPE) ** -0.5
    qa = qa_ref[0]
    qp = qp_ref[0]
    ckv = ckv_buf[slot].astype(BF16)
    kpe_t = kpe_buf[slot].astype(BF16)
    s = (_dot_nt(qa, ckv) + _dot(qp, kpe_t)) * scale
    c_new = cn_ref[0].astype(BF16).astype(F32)
    p_new_k = pn_ref[0].astype(BF16).astype(F32)
    s_new = (jnp.sum(qa.astype(F32) * c_new, axis=1, keepdims=True)
             + jnp.sum(qp.astype(F32) * p_new_k, axis=1, keepdims=True)) * scale
    m = jnp.maximum(jnp.max(s, axis=1, keepdims=True), s_new)
    p = jnp.exp(s - m)
    p_new = jnp.exp(s_new - m)
    denom = jnp.sum(p, axis=1, keepdims=True) + p_new
    num = _dot(p.astype(BF16), ckv) + p_new.astype(BF16).astype(F32) * c_new
    o_ref[0] = (num / denom).astype(o_ref.dtype)


def mla_decode(page_table, qa, qp, ckv_new, kpe_new, ckv_pool, kpe_pool, layer):
    b, n_pages = page_table.shape
    past = n_pages * PAGE_SIZE
    any_spec = pl.BlockSpec(memory_space=pl.ANY)
    grid_spec = pltpu.PrefetchScalarGridSpec(
        num_scalar_prefetch=1,
        grid=(b,),
        in_specs=[pl.BlockSpec((1, D_HEADS, D_KV_LORA), lambda i, pt: (i, 0, 0)),
                  pl.BlockSpec((1, D_HEADS, D_ROPE), lambda i, pt: (i, 0, 0)),
                  pl.BlockSpec((1, 1, D_KV_LORA), lambda i, pt: (i, 0, 0)),
                  pl.BlockSpec((1, 1, D_ROPE), lambda i, pt: (i, 0, 0)),
                  any_spec, any_spec],
        out_specs=pl.BlockSpec((1, D_HEADS, D_KV_LORA), lambda i, pt: (i, 0, 0)),
        scratch_shapes=[pltpu.VMEM((2, past, D_KV_LORA), F32),
                        pltpu.VMEM((2, D_ROPE, past), F32),
                        pltpu.SemaphoreType.DMA((2, 2))])
    return pl.pallas_call(
        functools.partial(_mla_decode_kernel, layer=layer, n_pages=n_pages),
        grid_spec=grid_spec,
        out_shape=jax.ShapeDtypeStruct((b, D_HEADS, D_KV_LORA), BF16),
        compiler_params=_cparams(("arbitrary",)),
        name="mla_decode",
    )(page_table, qa, qp, ckv_new, kpe_new, ckv_pool, kpe_pool)


def _rmsnorm(x, g, eps=EPS):
    return x * lax.rsqrt(jnp.mean(x * x, axis=-1, keepdims=True) + eps) * g


def _rope(x, pos):
    half = x.shape[-1] // 2
    inv = ROPE_THETA ** (-jnp.arange(half, dtype=F32) / half)
    ang = pos.astype(F32)[:, None] * inv[None, :]
    shape = (x.shape[0],) + (1,) * (x.ndim - 2) + (half,)
    cos = jnp.cos(ang).reshape(shape)
    sin = jnp.sin(ang).reshape(shape)
    x1, x2 = x[..., :half], x[..., half:]
    return jnp.concatenate([x1 * cos - x2 * sin, x2 * cos + x1 * sin], axis=-1)


def _layer(h, layer, wts, small, cache, page_table, state, dims):
    bp, tp, bs = dims
    p_rows = bp * tp
    m = h.shape[0]
    prm = {name: arr[layer] for name, arr in small.items()}
    pos = jnp.concatenate([jnp.tile(jnp.arange(tp, dtype=I32), bp),
                           jnp.full((bs,), page_table.shape[1] * PAGE_SIZE, I32)])

    xn, z = norm_matmul(h, wts['norm_mix_g'], wts['w_mix'], layer, tn=640, emit_xn=True)
    o1 = A_IN
    o2 = o1 + B_IN
    o3 = o2 + C_PAD
    za, zb, zc, zd = z[:, :o1], z[:, o1:o2], z[:, o2:o2 + C_IN], z[:, o3:o3 + D_IN]

    za_p = za[:p_rows].reshape(bp, tp, A_IN)
    prev_p = jnp.concatenate([jnp.zeros((bp, 1, A_IN), F32), za_p[:, :-1]], axis=1).reshape(p_rows, A_IN)
    z_prev = jnp.concatenate([prev_p, state['shift']], axis=0)
    zs = za + (z_prev - za) * prm['a_mu']
    r, k, v = zs[:, :BR], zs[:, BR:2 * BR], zs[:, 2 * BR:3 * BR]
    xw, xa = zs[:, 3 * BR:3 * BR + A_DECAY_LORA], zs[:, 3 * BR + A_DECAY_LORA:]
    lora = matmul(jnp.concatenate([jnp.tanh(xw), xa], axis=1).astype(BF16), wts['a_lora'], layer, tn=2 * BR)
    decay = jnp.exp(-DECAY_SCALE * jax.nn.sigmoid(prm['a_w0'] + lora[:, :BR]))
    a = jax.nn.sigmoid(prm['a_a0'] + lora[:, BR:])
    hd = lambda x: x.reshape(m, A_HEADS, A_HD)
    kk = hd(k * prm['a_kk'])
    kk = kk / jnp.maximum(jnp.sqrt(jnp.sum(kk * kk, axis=-1, keepdims=True)), 1e-12)
    k2 = hd(k * (1.0 + (a - 1.0) * prm['a_ka']))
    r3, v3, w3, a3 = hd(r), hd(v), hd(decay), hd(a)
    bonus = jnp.sum(r3 * k2 * prm['a_rk'], axis=-1, keepdims=True) * v3

    def scan_group(sl, b, t, s0):
        tr = lambda x: jnp.transpose(x[sl].reshape(b, t, A_HEADS, A_HD), (0, 2, 1, 3))
        y, s_t = rwkv_scan(tr(r3), tr(w3), tr(k2), tr(v3), tr(-kk), tr(kk * a3), s0)
        return jnp.transpose(y, (0, 2, 1, 3)).reshape(b * t, A_HEADS, A_HD), s_t

    y_p, s_p = scan_group(slice(0, p_rows), bp, tp, jnp.zeros((bp, A_HEADS, A_HD, A_HD), F32))
    y_s, s_s = scan_group(slice(p_rows, m), bs, 1, state['rwkv'])
    y = jnp.concatenate([y_p, y_s], axis=0)
    mu = jnp.mean(y, axis=-1, keepdims=True)
    var = jnp.mean(jnp.square(y - mu), axis=-1, keepdims=True)
    y = ((y - mu) * lax.rsqrt(var + A_LN_EPS) * prm['a_lnx_g'].reshape(A_HEADS, A_HD)
         + prm['a_lnx_b'].reshape(A_HEADS, A_HD))
    o_a = (y + bonus).reshape(m, BR)
    shift_p = za_p[:, -1]
    shift_s = za[p_rows:]

    ob_p = chunk_mlp(zb[:p_rows].reshape(bp, tp, B_IN), prm['b_ln_g'], prm['b_ln_b'],
                     prm['b_ws'], prm['b_bs']).reshape(p_rows, BR)
    zb_s = zb[p_rows:]
    xs = zb_s[:, BR:]
    mu = jnp.mean(xs, axis=-1, keepdims=True)
    var = jnp.mean(jnp.square(xs - mu), axis=-1, keepdims=True)
    v_rows = (xs - mu) * lax.rsqrt(var + EPS) * prm['b_ln_g'] + prm['b_ln_b']
    w00 = jnp.repeat(prm['b_ws'][:, 0, 0], B_GC)
    b00 = jnp.repeat(prm['b_bs'][:, 0], B_GC)
    ob_s = zb_s[:, :BR] * (v_rows.astype(BF16).astype(F32) * w00.astype(BF16).astype(F32) + b00)
    o_b = jnp.concatenate([ob_p, ob_s], axis=0)

    c1 = C_HEADS * C_HD
    c2 = c1 + C_KV_HEADS * C_HD
    c3 = c2 + C_KV_HEADS * C_HD
    c4 = c3 + C_IDX_HEADS * C_IDX_HD
    c5 = c4 + C_IDX_HD
    q = _rope(_rmsnorm(zc[:, :c1].reshape(m, C_HEADS, C_HD), prm['c_qn_g']), pos)
    kc = _rope(_rmsnorm(zc[:, c1:c2].reshape(m, C_KV_HEADS, C_HD), prm['c_kn_g']), pos)
    vc = zc[:, c2:c3].reshape(m, C_KV_HEADS, C_HD)
    qi = _rope(zc[:, c3:c4].reshape(m, C_IDX_HEADS, C_IDX_HD), pos)
    ki = _rope(zc[:, c4:c5], pos)
    wi = zc[:, c5:] * (C_IDX_HEADS ** -0.5)
    q_b, qi_b = q.astype(BF16), qi.astype(BF16)
    kvp = lambda x: jnp.transpose(x[:p_rows].reshape(bp, tp, C_KV_HEADS, C_HD), (0, 2, 1, 3)).astype(BF16)
    oc_p = dsa_prompt(jnp.transpose(q_b[:p_rows], (1, 0, 2)), jnp.transpose(qi_b[:p_rows], (1, 0, 2)),
                      wi[:p_rows], ki[:p_rows].reshape(bp, tp, C_IDX_HD).astype(BF16), kvp(kc), kvp(vc))
    oc_p = jnp.transpose(oc_p, (1, 0, 2)).reshape(p_rows, BR)
    kvw = C_KV_HEADS * C_HD
    past = page_table.shape[1] * PAGE_SIZE
    keys = dsa_decode_keys(page_table, qi_b[p_rows:], wi[p_rows:, :, None], ki[p_rows:, None, :],
                           cache['kidx'], layer)
    thr, cut = dsa_select(keys, min(C_TOPK, (past + 1) // 4))
    key_rows = jnp.transpose(keys, (1, 0, 2)).reshape(bs, 1, -1)
    oc_s = dsa_decode_attn(page_table, q_b[p_rows:], key_rows, thr[:, :, None], cut[:, :, None],
                           kc[p_rows:].reshape(bs, 1, kvw), vc[p_rows:].reshape(bs, 1, kvw),
                           cache['k'], cache['v'], layer)
    o_c = jnp.concatenate([oc_p, oc_s.reshape(bs, BR)], axis=0)

    d1 = D_Q_LORA
    d2 = d1 + D_KV_LORA
    cq = _rmsnorm(zd[:, :d1], prm['d_qa_norm_g'])
    qd = matmul(cq.astype(BF16), wts['d_w_uq'], layer, tn=D_HEADS * (D_NOPE + D_ROPE))
    qd = _rmsnorm(qd.reshape(m, D_HEADS, D_NOPE + D_ROPE), prm['d_qn_g'])
    q_nope = qd[..., :D_NOPE].reshape(m, D_HEADS * D_NOPE)
    q_pe = jnp.transpose(_rope(qd[..., D_NOPE:], pos), (1, 0, 2)).astype(BF16)
    ckv = _rmsnorm(zd[:, d1:d2], prm['d_kva_norm_g'])
    kpe = _rope(_rmsnorm(zd[:, d2:], prm['d_kn_g']), pos)
    q_abs = headmm_to_headmajor(q_nope.astype(BF16), wts['d_w_uk'], layer, out_dtype=BF16)
    ol_p = mla_prompt(q_abs[:, :p_rows], q_pe[:, :p_rows],
                      ckv[:p_rows].reshape(bp, tp, D_KV_LORA).astype(BF16),
                      kpe[:p_rows].reshape(bp, tp, D_ROPE).astype(BF16))
    ol_s = mla_decode(page_table, jnp.transpose(q_abs[:, p_rows:], (1, 0, 2)),
                      jnp.transpose(q_pe[:, p_rows:], (1, 0, 2)),
                      ckv[p_rows:, None, :], kpe[p_rows:, None, :], cache['ckv'], cache['kpe'], layer)
    o_lat = jnp.concatenate([ol_p, jnp.transpose(ol_s, (1, 0, 2))], axis=1)
    o_d = headmm_from_headmajor(o_lat, wts['d_w_uv'], layer, out_dtype=BF16)

    br = jnp.stack([o_a.astype(BF16), o_b.astype(BF16), o_c.astype(BF16), o_d], axis=0)
    merged = gate_merge(xn, br, wts['w_gate'], wts['w_branch'], layer)
    h = matmul_residual(merged, wts['w_out'], layer, h)

    up = norm_matmul(h, wts['norm_ffn_g'], wts['f_up'], layer, tn=512, emit_xn=False)
    act = conv_swiglu(up, jnp.transpose(state['conv'], (1, 0, 2)), wts['f_cw'], wts['f_cb'], layer,
                      t_prompt=tp, n_sample=bs)
    conv_p = jnp.stack([up[(b + 1) * tp - 2:(b + 1) * tp] for b in range(bp)], axis=0)
    conv_s = jnp.stack([state['conv'][:, 1], up[p_rows:]], axis=1)
    h = matmul_residual(act, wts['f_down'], layer, h)

    h = ple_update(h, wts['norm_ple_g'], wts['w_ple_gate'], wts['pe'], wts['w_ple_proj'], layer)

    st_p = (s_p, shift_p, conv_p, kc[:p_rows].reshape(bp, tp, C_KV_HEADS, C_HD),
            vc[:p_rows].reshape(bp, tp, C_KV_HEADS, C_HD), ki[:p_rows].reshape(bp, tp, C_IDX_HD),
            ckv[:p_rows].reshape(bp, tp, D_KV_LORA), kpe[:p_rows].reshape(bp, tp, D_ROPE))
    st_s = (s_s, shift_s, conv_s, kc[p_rows:].reshape(bs, 1, C_KV_HEADS, C_HD),
            vc[p_rows:].reshape(bs, 1, C_KV_HEADS, C_HD), ki[p_rows:].reshape(bs, 1, C_IDX_HD),
            ckv[p_rows:].reshape(bs, 1, D_KV_LORA), kpe[p_rows:].reshape(bs, 1, D_ROPE),
            v_rows.reshape(bs, 1, BR))
    return h, st_p, st_s


def kernel(x_prompt, x_sample, state_rwkv, state_rwkv_shift, state_ffn_conv, cache_dsa_k, cache_dsa_v, cache_dsa_kidx, cache_mla_ckv, cache_mla_kpe, page_table, p_prompt, p_sample, norm_mix_g, w_in, a_mu, a_w0, a_w2, a_a0, a_a2, a_kk, a_ka, a_rk, a_lnx_g, a_lnx_b, b_ln_g, b_ln_b, b_ws, b_bs, c_qn_g, c_kn_g, d_qa_norm_g, d_w_uq, d_qn_g, d_kva_norm_g, d_kn_g, d_w_uk, d_w_uv, w_branch, w_out, norm_ffn_g, f_up, f_cw, f_cb, f_down, norm_ple_g, w_ple_gate, w_ple_proj):
    bp, tp, _ = x_prompt.shape
    bs = x_sample.shape[0]
    depth = w_in.shape[0]
    p_rows = bp * tp
    h = jnp.concatenate([x_prompt.reshape(p_rows, D_MODEL), x_sample.reshape(bs, D_MODEL)], axis=0)
    o2 = A_IN + B_IN
    o3 = o2 + C_IN
    o4 = o3 + D_IN
    w_in_b = w_in.astype(BF16)
    pad = lambda x, n: jnp.pad(x, ((0, 0), (0, 0), (0, n - x.shape[2])))
    zeros = jnp.zeros((depth, A_DECAY_LORA, BR), F32)
    gain = lambda g: g.reshape(depth, 1, D_MODEL)
    wts = dict(
        w_mix=jnp.concatenate([w_in_b[:, :, :o2], pad(w_in_b[:, :, o2:o3], C_PAD), pad(w_in_b[:, :, o3:o4], D_PAD)], axis=2),
        w_gate=w_in_b[:, :, o4:],
        a_lora=jnp.concatenate([jnp.concatenate([a_w2, zeros], axis=2),
                                jnp.concatenate([zeros, a_a2], axis=2)], axis=1).astype(BF16),
        d_w_uq=d_w_uq.astype(BF16),
        d_w_uk=jnp.transpose(d_w_uk, (0, 2, 3, 1)).astype(BF16),
        d_w_uv=jnp.transpose(d_w_uv, (0, 2, 1, 3)).astype(BF16),
        w_branch=w_branch.astype(BF16), w_out=w_out.astype(BF16),
        f_up=f_up.astype(BF16), f_down=f_down.astype(BF16),
        w_ple_gate=w_ple_gate.astype(BF16), w_ple_proj=w_ple_proj.astype(BF16),
        norm_mix_g=gain(norm_mix_g), norm_ffn_g=gain(norm_ffn_g), norm_ple_g=gain(norm_ple_g),
        pe=jnp.concatenate([p_prompt.reshape(depth, p_rows, -1), p_sample.reshape(depth, bs, -1)], axis=1).astype(BF16))
    small = dict(a_mu=a_mu, a_w0=a_w0, a_a0=a_a0, a_kk=a_kk, a_ka=a_ka, a_rk=a_rk, a_lnx_g=a_lnx_g,
                 a_lnx_b=a_lnx_b, b_ln_g=b_ln_g, b_ln_b=b_ln_b, b_ws=b_ws, b_bs=b_bs, c_qn_g=c_qn_g,
                 c_kn_g=c_kn_g, d_qa_norm_g=d_qa_norm_g, d_qn_g=d_qn_g, d_kva_norm_g=d_kva_norm_g,
                 d_kn_g=d_kn_g)
    wts.update(f_cw=f_cw, f_cb=f_cb.reshape(depth, 1, -1))
    kvw = C_KV_HEADS * C_HD
    cache = dict(k=jnp.transpose(cache_dsa_k, (0, 1, 3, 4, 2)),
                 v=jnp.transpose(cache_dsa_v, (0, 1, 3, 4, 2)),
                 kidx=jnp.swapaxes(cache_dsa_kidx, 2, 3), ckv=cache_mla_ckv,
                 kpe=jnp.swapaxes(cache_mla_kpe, 2, 3))
    sp, ss = [], []
    for i in range(depth):
        state = dict(rwkv=state_rwkv[i], shift=state_rwkv_shift[i], conv=state_ffn_conv[i])
        h, st_p, st_s = _layer(h, i, wts, small, cache, page_table, state, (bp, tp, bs))
        sp.append(st_p)
        ss.append(st_s)

    def stk(lst, j):
        return jnp.stack([s[j] for s in lst], axis=0)

    return ((h[:p_rows].reshape(bp, tp, D_MODEL), h[p_rows:].reshape(bs, 1, D_MODEL))
            + tuple(stk(sp, j) for j in range(8)) + tuple(stk(ss, j) for j in range(9)))
```

```python
import functools

import jax
import jax.numpy as jnp
from jax import lax
from jax.experimental import pallas as pl
from jax.experimental.pallas import tpu as pltpu

F32 = jnp.float32
BF16 = jnp.bfloat16
I32 = jnp.int32

D_MODEL = 2048
N_BRANCH = 4
BR = D_MODEL // N_BRANCH
EPS = 1e-6
ROPE_THETA = 10000.0
Q_BLOCK = 128
A_HD = 64
A_HEADS = BR // A_HD
A_DECAY_LORA = 64
A_AAA_LORA = 64
A_IN = 3 * BR + A_DECAY_LORA + A_AAA_LORA
A_LN_EPS = 64e-5
DECAY_SCALE = 0.606531
CHUNK = 128
B_GROUPS = 4
B_GC = BR // B_GROUPS
B_IN = 2 * BR
C_HD = 64
C_HEADS = BR // C_HD
C_KV_HEADS = 2
C_IDX_HEADS = 4
C_IDX_HD = 64
C_TOPK = 256
C_IN = C_HEADS * C_HD + 2 * C_KV_HEADS * C_HD + C_IDX_HEADS * C_IDX_HD + C_IDX_HD + C_IDX_HEADS
D_NOPE = 64
D_ROPE = 32
D_V = 64
D_HEADS = BR // D_V
D_Q_LORA = 3 * D_MODEL // 16
D_KV_LORA = D_MODEL // 16
D_IN = D_Q_LORA + D_KV_LORA + D_ROPE
PAGE_SIZE = 128

LANE = 128
C_PAD = 1152
D_PAD = 640
N_MIX = A_IN + B_IN + C_PAD + D_PAD
VMEM_LIMIT = 56 * 1024 * 1024
ROW_TILE = 1040
KEY_CHUNK = 512
DEC_SEQS = 8
NEG = -1e30
INT_MIN = -2147483648


def _pick_tile(n, target, mult=8):
    best = n
    for t in range(mult, min(n, target) + 1, mult):
        if n % t == 0:
            best = t
    return best


def _cparams(sem):
    return pltpu.CompilerParams(dimension_semantics=sem, vmem_limit_bytes=VMEM_LIMIT)


def _rowmax(x):
    part = x[:, 0:LANE]
    for u in range(1, x.shape[1] // LANE):
        part = jnp.maximum(part, x[:, u * LANE:(u + 1) * LANE])
    return jnp.max(part, axis=1, keepdims=True)


def _rowsum(x):
    part = x[:, 0:LANE]
    for u in range(1, x.shape[1] // LANE):
        part = part + x[:, u * LANE:(u + 1) * LANE]
    return jnp.sum(part, axis=1, keepdims=True)


def _dot(a, b):
    return jnp.dot(a, b, preferred_element_type=F32)


def _dot_nt(a, b):
    return lax.dot_general(a, b, (((1,), (1,)), ((), ())), preferred_element_type=F32)


def _rms_rows(h_ref, g_ref):
    x = h_ref[...]
    ms = jnp.mean(x * x, axis=-1, keepdims=True)
    return (x * lax.rsqrt(ms + EPS) * g_ref[...]).astype(BF16)


def _norm_mm_kernel(h_ref, g_ref, w_ref, *rest, emit_xn):
    if emit_xn:
        xn_ref, z_ref, xs_ref = rest
    else:
        z_ref, xs_ref = rest

    @pl.when(pl.program_id(1) == 0)
    def _():
        xn = _rms_rows(h_ref, g_ref)
        xs_ref[...] = xn
        if emit_xn:
            xn_ref[...] = xn

    z_ref[...] = _dot(xs_ref[...], w_ref[...])


def norm_matmul(h, g, w, layer, *, tn, emit_xn, tm_target=ROW_TILE):
    m, k = h.shape
    n = w.shape[2]
    tm = _pick_tile(m, tm_target)
    out_shape = [jax.ShapeDtypeStruct((m, n), F32)]
    out_specs = [pl.BlockSpec((tm, tn), lambda i, j: (i, j))]
    if emit_xn:
        out_shape.insert(0, jax.ShapeDtypeStruct((m, k), BF16))
        out_specs.insert(0, pl.BlockSpec((tm, k), lambda i, j: (i, 0)))
    outs = pl.pallas_call(
        functools.partial(_norm_mm_kernel, emit_xn=emit_xn),
        grid=(m // tm, n // tn),
        in_specs=[pl.BlockSpec((tm, k), lambda i, j: (i, 0)),
                  pl.BlockSpec((None, 1, k), lambda i, j: (layer, 0, 0)),
                  pl.BlockSpec((None, k, tn), lambda i, j: (layer, 0, j))],
        out_specs=out_specs,
        out_shape=out_shape,
        scratch_shapes=[pltpu.VMEM((tm, k), BF16)],
        compiler_params=_cparams(("parallel", "arbitrary")),
        name="norm_matmul",
    )(h, g, w)
    return outs if emit_xn else outs[0]


def _gate_merge_kernel(xn_ref, br_ref, wg_ref, wb_ref, o_ref, acc_ref):
    kk = pl.program_id(2)
    gate = jax.nn.sigmoid(_dot(xn_ref[...], wg_ref[...]))
    up = _dot(br_ref[0], wb_ref[...])
    contrib = gate * up

    @pl.when(kk == 0)
    def _():
        acc_ref[...] = contrib

    @pl.when(kk > 0)
    def _():
        acc_ref[...] += contrib

    @pl.when(kk == N_BRANCH - 1)
    def _():
        o_ref[...] = acc_ref[...].astype(BF16)


def gate_merge(xn, br, wg, wb, layer, *, tn=512, tm_target=ROW_TILE):
    m, k = xn.shape
    tm = _pick_tile(m, tm_target)
    nj = D_MODEL // tn
    return pl.pallas_call(
        _gate_merge_kernel,
        grid=(m // tm, nj, N_BRANCH),
        in_specs=[pl.BlockSpec((tm, k), lambda i, j, c: (i, 0)),
                  pl.BlockSpec((1, tm, BR), lambda i, j, c: (c, i, 0)),
                  pl.BlockSpec((None, k, tn), lambda i, j, c: (layer, 0, c * nj + j)),
                  pl.BlockSpec((None, None, BR, tn), lambda i, j, c: (layer, c, 0, j))],
        out_specs=pl.BlockSpec((tm, tn), lambda i, j, c: (i, j)),
        out_shape=jax.ShapeDtypeStruct((m, D_MODEL), BF16),
        scratch_shapes=[pltpu.VMEM((tm, tn), F32)],
        compiler_params=_cparams(("parallel", "parallel", "arbitrary")),
        name="gate_merge",
    )(xn, br, wg, wb)


def _mm_res_kernel(x_ref, w_ref, r_ref, o_ref):
    o_ref[...] = r_ref[...] + _dot(x_ref[...], w_ref[...])


def matmul_residual(x, w, layer, res, *, tn=512, tm_target=ROW_TILE):
    m, k = x.shape
    n = w.shape[2]
    tm = _pick_tile(m, tm_target)
    return pl.pallas_call(
        _mm_res_kernel,
        grid=(m // tm, n // tn),
        in_specs=[pl.BlockSpec((tm, k), lambda i, j: (i, 0)),
                  pl.BlockSpec((None, k, tn), lambda i, j: (layer, 0, j)),
                  pl.BlockSpec((tm, tn), lambda i, j: (i, j))],
        out_specs=pl.BlockSpec((tm, tn), lambda i, j: (i, j)),
        out_shape=jax.ShapeDtypeStruct((m, n), F32),
        compiler_params=_cparams(("parallel", "parallel")),
        name="matmul_residual",
    )(x, w, res)


def _mm_kernel(x_ref, w_ref, o_ref):
    o_ref[...] = _dot(x_ref[...], w_ref[...]).astype(o_ref.dtype)


def matmul(x, w, layer, *, tn, out_dtype=F32, tm_target=ROW_TILE):
    m, k = x.shape
    n = w.shape[2]
    tm = _pick_tile(m, tm_target)
    return pl.pallas_call(
        _mm_kernel,
        grid=(m // tm, n // tn),
        in_specs=[pl.BlockSpec((tm, k), lambda i, j: (i, 0)),
                  pl.BlockSpec((None, k, tn), lambda i, j: (layer, 0, j))],
        out_specs=pl.BlockSpec((tm, tn), lambda i, j: (i, j)),
        out_shape=jax.ShapeDtypeStruct((m, n), out_dtype),
        compiler_params=_cparams(("parallel", "parallel")),
        name="matmul",
    )(x, w)


def _ple_kernel(h_ref, g_ref, wg_ref, pe_ref, wp_ref, r_ref, o_ref, xs_ref):
    @pl.when(pl.program_id(1) == 0)
    def _():
        xs_ref[...] = _rms_rows(h_ref, g_ref)

    gate = jax.nn.sigmoid(_dot(xs_ref[...], wg_ref[...]))
    o_ref[...] = r_ref[...] + gate * _dot(pe_ref[...], wp_ref[...])


def ple_update(h, g, wg, pe, wp, layer, *, tn=512, tm_target=ROW_TILE):
    m, k = h.shape
    kp = pe.shape[2]
    tm = _pick_tile(m, tm_target)
    return pl.pallas_call(
        _ple_kernel,
        grid=(m // tm, D_MODEL // tn),
        in_specs=[pl.BlockSpec((tm, k), lambda i, j: (i, 0)),
                  pl.BlockSpec((None, 1, k), lambda i, j: (layer, 0, 0)),
                  pl.BlockSpec((None, k, tn), lambda i, j: (layer, 0, j)),
                  pl.BlockSpec((None, tm, kp), lambda i, j: (layer, i, 0)),
                  pl.BlockSpec((None, kp, tn), lambda i, j: (layer, 0, j)),
                  pl.BlockSpec((tm, tn), lambda i, j: (i, j))],
        out_specs=pl.BlockSpec((tm, tn), lambda i, j: (i, j)),
        out_shape=jax.ShapeDtypeStruct((m, D_MODEL), F32),
        scratch_shapes=[pltpu.VMEM((tm, k), BF16)],
        compiler_params=_cparams(("parallel", "arbitrary")),
        name="ple_update",
    )(h, g, wg, pe, wp, h)


def _headmm_out_hm_kernel(x_ref, w_ref, o_ref, *, heads, din):
    for hh in range(heads):
        o_ref[hh] = _dot(x_ref[:, hh * din:(hh + 1) * din], w_ref[hh]).astype(o_ref.dtype)


def headmm_to_headmajor(x, w, layer, *, out_dtype, tm_target=ROW_TILE):
    m = x.shape[0]
    _, heads, din, dout = w.shape
    tm = _pick_tile(m, tm_target)
    return pl.pallas_call(
        functools.partial(_headmm_out_hm_kernel, heads=heads, din=din),
        grid=(m // tm,),
        in_specs=[pl.BlockSpec((tm, heads * din), lambda i: (i, 0)),
                  pl.BlockSpec((None, heads, din, dout), lambda i: (layer, 0, 0, 0))],
        out_specs=pl.BlockSpec((heads, tm, dout), lambda i: (0, i, 0)),
        out_shape=jax.ShapeDtypeStruct((heads, m, dout), out_dtype),
        compiler_params=_cparams(("parallel",)),
        name="headmm_to_headmajor",
    )(x, w)


def _headmm_in_hm_kernel(x_ref, w_ref, o_ref, *, heads, dout):
    for hh in range(heads):
        o_ref[:, hh * dout:(hh + 1) * dout] = _dot(x_ref[hh], w_ref[hh]).astype(o_ref.dtype)


def headmm_from_headmajor(x, w, layer, *, out_dtype, tm_target=ROW_TILE):
    heads, m, din = x.shape
    dout = w.shape[3]
    tm = _pick_tile(m, tm_target)
    return pl.pallas_call(
        functools.partial(_headmm_in_hm_kernel, heads=heads, dout=dout),
        grid=(m // tm,),
        in_specs=[pl.BlockSpec((heads, tm, din), lambda i: (0, i, 0)),
                  pl.BlockSpec((None, heads, din, dout), lambda i: (layer, 0, 0, 0))],
        out_specs=pl.BlockSpec((tm, heads * dout), lambda i: (i, 0)),
        out_shape=jax.ShapeDtypeStruct((m, heads * dout), out_dtype),
        compiler_params=_cparams(("parallel",)),
        name="headmm_from_headmajor",
    )(x, w)


def _conv_swiglu_kernel(ua_ref, ub_ref, ha_ref, hb_ref, sa_ref, sb_ref, cwa_ref, cwb_ref, cba_ref, cbb_ref,
                        o_ref, *, tm, t_prompt, n_prompt_blocks):
    i = pl.program_id(1)
    is_sample = i >= n_prompt_blocks
    seq_start = (i * tm) % t_prompt == 0
    r = lax.broadcasted_iota(I32, ua_ref.shape, 0)

    def conv(u_ref, h_ref, s_ref, cw_ref, cb_ref):
        x = u_ref[...]
        h1 = jnp.where(seq_start, 0.0, h_ref[7:8, :])
        h2 = jnp.where(seq_start, 0.0, h_ref[6:7, :])
        x1 = jnp.where(r >= 1, pltpu.roll(x, 1, axis=0), h1)
        x2 = jnp.where(r >= 2, pltpu.roll(x, 2, axis=0), jnp.where(r == 1, h1, h2))
        x1 = jnp.where(is_sample, s_ref[1], x1)
        x2 = jnp.where(is_sample, s_ref[0], x2)
        return cb_ref[...] + x2 * cw_ref[0:1, :] + x1 * cw_ref[1:2, :] + x * cw_ref[2:3, :]

    ca = conv(ua_ref, ha_ref, sa_ref, cwa_ref, cba_ref)
    cb = conv(ub_ref, hb_ref, sb_ref, cwb_ref, cbb_ref)
    o_ref[...] = (ca * jax.nn.sigmoid(ca) * cb).astype(o_ref.dtype)


def conv_swiglu(up, conv0, cw, cb, layer, *, t_prompt, n_sample, tn=2816):
    m, f2 = up.shape
    f = f2 // 2
    tm = _pick_tile(n_sample, 128)
    p_rows = m - n_sample
    assert p_rows % tm == 0 and t_prompt % tm == 0 and tm % 8 == 0
    npb = p_rows // tm
    nj = f // tn
    hb = tm // 8
    main_a = pl.BlockSpec((tm, tn), lambda j, i: (i, j))
    main_b = pl.BlockSpec((tm, tn), lambda j, i: (i, j + nj))
    halo_a = pl.BlockSpec((8, tn), lambda j, i: (jnp.maximum(i * hb - 1, 0), j))
    halo_b = pl.BlockSpec((8, tn), lambda j, i: (jnp.maximum(i * hb - 1, 0), j + nj))
    st_a = pl.BlockSpec((2, tm, tn), lambda j, i: (0, jnp.maximum(i - npb, 0), j))
    st_b = pl.BlockSpec((2, tm, tn), lambda j, i: (0, jnp.maximum(i - npb, 0), j + nj))
    cw_a = pl.BlockSpec((None, 3, tn), lambda j, i: (layer, 0, j))
    cw_b = pl.BlockSpec((None, 3, tn), lambda j, i: (layer, 0, j + nj))
    cb_a = pl.BlockSpec((None, 1, tn), lambda j, i: (layer, 0, j))
    cb_b = pl.BlockSpec((None, 1, tn), lambda j, i: (layer, 0, j + nj))
    return pl.pallas_call(
        functools.partial(_conv_swiglu_kernel, tm=tm, t_prompt=t_prompt, n_prompt_blocks=npb),
        grid=(nj, m // tm),
        in_specs=[main_a, main_b, halo_a, halo_b, st_a, st_b, cw_a, cw_b, cb_a, cb_b],
        out_specs=pl.BlockSpec((tm, tn), lambda j, i: (i, j)),
        out_shape=jax.ShapeDtypeStruct((m, f), BF16),
        compiler_params=_cparams(("parallel", "parallel")),
        name="conv_swiglu",
    )(up, up, up, up, conv0, conv0, cw, cw, cb, cb)


def _rwkv_scan_kernel(r_ref, w_ref, k_ref, v_ref, nkk_ref, bb_ref, s0_ref, y_ref, sT_ref, s_ref, *, tc, nb):
    c = pl.program_id(1)

    @pl.when(c == 0)
    def _():
        s_ref[...] = s0_ref[...]

    row = lax.broadcasted_iota(I32, (A_HD, A_HD), 0)
    col = lax.broadcasted_iota(I32, (A_HD, A_HD), 1)
    eye = (row == col).astype(F32)

    def step(t, carry):
        sl = (slice(None), slice(None), pl.ds(t, 1), slice(None))
        s_prev = s_ref[...]
        sa = jnp.sum(s_prev * nkk_ref[sl], axis=-1, keepdims=True)
        vr = v_ref[sl].reshape(nb * A_HEADS, 1, A_HD)
        hi = vr.astype(BF16).astype(F32)
        mid = (vr - hi).astype(BF16).astype(F32)
        lo = (vr - hi - mid).astype(BF16).astype(F32)
        zero = jnp.zeros((nb * A_HEADS, 5, A_HD), F32)
        a3 = jnp.concatenate([hi, mid, lo, zero], axis=1).astype(BF16)
        ones = (lax.broadcasted_iota(I32, (nb * A_HEADS, 8, A_HD), 1) < 3).astype(BF16)
        v_col = lax.dot_general(a3, ones, (((1,), (1,)), ((0,), (0,))), preferred_element_type=F32)
        v_col = v_col.reshape(nb, A_HEADS, A_HD, A_HD)
        s_new = s_prev * w_ref[sl] + sa * bb_ref[sl] + v_col * k_ref[sl]
        s_ref[...] = s_new
        y_col = jnp.sum(s_new * r_ref[sl], axis=-1, keepdims=True)
        y_ref[sl] = jnp.sum(eye * y_col, axis=-2, keepdims=True)
        return carry

    lax.fori_loop(0, tc, step, 0)

    @pl.when(c == pl.num_programs(1) - 1)
    def _():
        sT_ref[...] = s_ref[...]


def rwkv_scan(r, w, k, v, nkk, bb, s0):
    b, hds, t, n = r.shape
    tc = _pick_tile(t, 256)
    nb = 2 if b % 2 == 0 else 1
    vec = pl.BlockSpec((nb, hds, tc, n), lambda i, c: (i, 0, c, 0))
    st = pl.BlockSpec((nb, hds, n, n), lambda i, c: (i, 0, 0, 0))
    return pl.pallas_call(
        functools.partial(_rwkv_scan_kernel, tc=tc, nb=nb),
        grid=(b // nb, t // tc),
        in_specs=[vec, vec, vec, vec, vec, vec, st],
        out_specs=[vec, st],
        out_shape=[jax.ShapeDtypeStruct((b, hds, t, n), F32), jax.ShapeDtypeStruct((b, hds, n, n), F32)],
        scratch_shapes=[pltpu.VMEM((nb, hds, n, n), F32)],
        compiler_params=_cparams(("parallel", "arbitrary")),
        name="rwkv_scan",
    )(r, w, k, v, nkk, bb, s0)


def _chunk_mlp_kernel(z_ref, lg_ref, lb_ref, ws_ref, bst_ref, o_ref):
    u = z_ref[0, :, :BR]
    x = z_ref[0, :, BR:]
    mu = jnp.mean(x, axis=-1, keepdims=True)
    var = jnp.mean(jnp.square(x - mu), axis=-1, keepdims=True)
    vn = (x - mu) * lax.rsqrt(var + EPS) * lg_ref[...] + lb_ref[...]
    tcn = ws_ref.shape[1]
    causal = lax.broadcasted_iota(I32, (tcn, tcn), 0) >= lax.broadcasted_iota(I32, (tcn, tcn), 1)
    for g in range(B_GROUPS):
        wmat = jnp.where(causal, ws_ref[g], 0.0).astype(BF16)
        s = _dot(wmat, vn[:, g * B_GC:(g + 1) * B_GC].astype(BF16)) + bst_ref[:, g:g + 1]
        o_ref[0, :, g * B_GC:(g + 1) * B_GC] = u[:, g * B_GC:(g + 1) * B_GC] * s


def chunk_mlp(zb, ln_g, ln_b, ws, bs):
    b, t, _ = zb.shape
    tc = min(t, CHUNK)
    return pl.pallas_call(
        _chunk_mlp_kernel,
        grid=(b, t // tc),
        in_specs=[pl.BlockSpec((1, tc, B_IN), lambda i, c: (i, c, 0)),
                  pl.BlockSpec((1, BR), lambda i, c: (0, 0)),
                  pl.BlockSpec((1, BR), lambda i, c: (0, 0)),
                  pl.BlockSpec((B_GROUPS, tc, tc), lambda i, c: (0, 0, 0)),
                  pl.BlockSpec((tc, B_GROUPS), lambda i, c: (0, 0))],
        out_specs=pl.BlockSpec((1, tc, BR), lambda i, c: (i, c, 0)),
        out_shape=jax.ShapeDtypeStruct((b, t, BR), F32),
        compiler_params=_cparams(("parallel", "parallel")),
        name="chunk_mlp",
    )(zb, ln_g.reshape(1, BR), ln_b.reshape(1, BR), ws[:, :tc, :tc], bs[:, :tc].T)


def _sortable_key(score):
    score = jnp.where(score == 0.0, 0.0, score)
    bits = pltpu.bitcast(score, I32)
    return jnp.where(bits < 0, bits ^ 0x7FFFFFFF, bits)


def _count_rows(key_ref, n_chunks, rows, wk, pred):
    def body(c, acc):
        hit = pred(c, key_ref[c]).astype(I32)
        part = hit[:, 0:LANE]
        for u in range(1, wk // LANE):
            part = part + hit[:, u * LANE:(u + 1) * LANE]
        return acc + part

    acc = lax.fori_loop(0, n_chunks, body, jnp.zeros((rows, LANE), I32))
    return jnp.sum(acc, axis=1, keepdims=True)


def _select_threshold(key_ref, n_chunks, rows, wk, n_sel, n_pos):
    s_off = lax.broadcasted_iota(I32, (rows, wk), 1)

    def bit_body(i, cur):
        bit = lax.shift_left(jnp.int32(1), jnp.int32(31) - i)
        trial = cur | bit
        th = trial ^ INT_MIN
        ok = _count_rows(key_ref, n_chunks, rows, wk, lambda c, kk: kk >= th) >= n_sel
        return jnp.where(ok, trial, cur)

    thr = lax.fori_loop(0, 32, bit_body, jnp.zeros((rows, 1), I32)) ^ INT_MIN
    n_ge = _count_rows(key_ref, n_chunks, rows, wk, lambda c, kk: kk >= thr)
    n_gt = _count_rows(key_ref, n_chunks, rows, wk, lambda c, kk: kk > thr)
    need = n_sel - n_gt
    tie_rows = jnp.where((n_ge > n_sel) & (thr != INT_MIN), 1, 0)
    any_tie = jnp.max(tie_rows)

    def tie_search():
        nbits = max(1, (n_pos - 1).bit_length())

        def body(i, lohi):
            lo, hi = lohi
            mid = lax.shift_right_arithmetic(lo + hi, 1)
            cnt = _count_rows(key_ref, n_chunks, rows, wk,
                              lambda c, kk: (kk == thr) & (c * wk + s_off <= mid))
            ok = cnt >= need
            return jnp.where(ok, lo, mid), jnp.where(ok, mid, hi)

        lo = jnp.full((rows, 1), -1, I32)
        hi = jnp.full((rows, 1), n_pos - 1, I32)
        return lax.fori_loop(0, nbits + 1, body, (lo, hi))[1]

    cut = lax.cond(any_tie > 0, tie_search, lambda: jnp.full((rows, 1), n_pos - 1, I32))
    return thr, cut


def _dsa_prompt_kernel(q_ref, qi_ref, wi_ref, ki_ref, k_ref, v_ref, o_ref,
                       key_ref, m_ref, l_ref, acc_ref, *, t_total, wk, n_sel):
    qb = pl.program_id(1)
    tq = Q_BLOCK
    n_chunks = (qb * tq + tq - 1) // wk + 1
    t_pos = qb * tq + lax.broadcasted_iota(I32, (tq, wk), 0)
    s_off = lax.broadcasted_iota(I32, (tq, wk), 1)

    def score_body(c, carry):
        k0 = pl.multiple_of(c * wk, wk)
        ki = ki_ref[0, pl.ds(k0, wk), :]
        sc = jnp.zeros((tq, wk), F32)
        for hh in range(C_IDX_HEADS):
            e = _dot_nt(qi_ref[hh], ki) * (C_IDX_HD ** -0.5)
            sc = sc + wi_ref[:, hh:hh + 1] * jnp.maximum(e, 0.0)
        key_ref[c] = jnp.where(k0 + s_off <= t_pos, _sortable_key(sc), INT_MIN)
        return carry

    lax.fori_loop(0, n_chunks, score_body, 0)
    thr, cut = _select_threshold(key_ref, n_chunks, tq, wk, n_sel, t_total)
    tie_ok = thr != INT_MIN

    m_ref[...] = jnp.full(m_ref.shape, NEG, F32)
    l_ref[...] = jnp.zeros(l_ref.shape, F32)
    acc_ref[...] = jnp.zeros(acc_ref.shape, F32)
    grp = C_HEADS // C_KV_HEADS
    rows = grp * tq

    def attn_body(c, carry):
        k0 = pl.multiple_of(c * wk, wk)
        kk = key_ref[c]
        sel = ((kk > thr) | ((kk == thr) & (k0 + s_off <= cut) & tie_ok))[None]
        for n in range(C_KV_HEADS):
            qn = q_ref[n * grp:(n + 1) * grp].reshape(rows, C_HD)
            kn = k_ref[0, n, pl.ds(k0, wk), :]
            vn = v_ref[0, n, pl.ds(k0, wk), :]
            s = _dot_nt(qn, kn) * (C_HD ** -0.5)
            s = jnp.where(sel, s.reshape(grp, tq, wk), NEG).reshape(rows, wk)
            rs = slice(n * rows, (n + 1) * rows)
            m_old = m_ref[rs]
            m_new = jnp.maximum(m_old, _rowmax(s))
            p = jnp.exp(s - jnp.tile(m_new, (1, wk // LANE)))
            alpha = jnp.exp(m_old - m_new)
            l_ref[rs] = alpha * l_ref[rs] + _rowsum(p)
            acc_ref[rs] = alpha[:, :C_HD] * acc_ref[rs] + _dot(p.astype(BF16), vn)
            m_ref[rs] = m_new
        return carry

    lax.fori_loop(0, n_chunks, attn_body, 0)
    o_ref[...] = (acc_ref[...] / l_ref[:, :C_HD]).reshape(C_HEADS, tq, C_HD)


def dsa_prompt(q, qi, wi, ki, k, v):
    b, t, _ = ki.shape
    tq = Q_BLOCK
    nq = t // tq
    wk = min(KEY_CHUNK, t)
    n_sel = min(C_TOPK, t // 4)
    kern = functools.partial(_dsa_prompt_kernel, t_total=t, wk=wk, n_sel=n_sel)
    return pl.pallas_call(
        kern,
        grid=(b, nq),
        in_specs=[pl.BlockSpec((C_HEADS, tq, C_HD), lambda i, j: (0, i * nq + j, 0)),
                  pl.BlockSpec((C_IDX_HEADS, tq, C_IDX_HD), lambda i, j: (0, i * nq + j, 0)),
                  pl.BlockSpec((tq, C_IDX_HEADS), lambda i, j: (i * nq + j, 0)),
                  pl.BlockSpec((1, t, C_IDX_HD), lambda i, j: (i, 0, 0)),
                  pl.BlockSpec((1, C_KV_HEADS, t, C_HD), lambda i, j: (i, 0, 0, 0)),
                  pl.BlockSpec((1, C_KV_HEADS, t, C_HD), lambda i, j: (i, 0, 0, 0))],
        out_specs=pl.BlockSpec((C_HEADS, tq, C_HD), lambda i, j: (0, i * nq + j, 0)),
        out_shape=jax.ShapeDtypeStruct((C_HEADS, b * t, C_HD), F32),
        scratch_shapes=[pltpu.VMEM((t // wk, tq, wk), I32),
                        pltpu.VMEM((C_HEADS * tq, LANE), F32),
                        pltpu.VMEM((C_HEADS * tq, LANE), F32),
                        pltpu.VMEM((C_HEADS * tq, C_HD), F32)],
        compiler_params=_cparams(("parallel", "arbitrary")),
        name="dsa_prompt",
    )(q, qi, wi, ki, k, v)


def _mla_prompt_kernel(qa_ref, qp_ref, ckv_ref, kpe_ref, o_ref, m_ref, l_ref, acc_ref, *, wk):
    qb = pl.program_id(1)
    tq = Q_BLOCK
    rows = D_HEADS * tq
    n_chunks = (qb * tq + tq - 1) // wk + 1
    qa = qa_ref[...].reshape(rows, D_KV_LORA)
    qp = qp_ref[...].reshape(rows, D_ROPE)
    t_pos = qb * tq + lax.broadcasted_iota(I32, (tq, wk), 0)
    s_off = lax.broadcasted_iota(I32, (tq, wk), 1)
    scale = (D_NOPE + D_ROPE) ** -0.5
    m_ref[...] = jnp.full(m_ref.shape, NEG, F32)
    l_ref[...] = jnp.zeros(l_ref.shape, F32)
    acc_ref[...] = jnp.zeros(acc_ref.shape, F32)

    def body(c, carry):
        k0 = pl.multiple_of(c * wk, wk)
        kc = ckv_ref[0, pl.ds(k0, wk), :]
        kp = kpe_ref[0, pl.ds(k0, wk), :]
        s = (_dot_nt(qa, kc) + _dot_nt(qp, kp)) * scale
        ok = (k0 + s_off <= t_pos)[None]
        s = jnp.where(ok, s.reshape(D_HEADS, tq, wk), NEG).reshape(rows, wk)
        m_old = m_ref[...]
        m_new = jnp.maximum(m_old, _rowmax(s))
        p = jnp.exp(s - jnp.tile(m_new, (1, wk // LANE)))
        alpha = jnp.exp(m_old - m_new)
        l_ref[...] = alpha * l_ref[...] + _rowsum(p)
        acc_ref[...] = alpha * acc_ref[...] + _dot(p.astype(BF16), kc)
        m_ref[...] = m_new
        return carry

    lax.fori_loop(0, n_chunks, body, 0)
    o_ref[...] = (acc_ref[...] / l_ref[...]).reshape(D_HEADS, tq, D_KV_LORA).astype(o_ref.dtype)


def mla_prompt(qa, qp, ckv, kpe):
    b, t, _ = ckv.shape
    tq = Q_BLOCK
    nq = t // tq
    wk = min(KEY_CHUNK, t)
    return pl.pallas_call(
        functools.partial(_mla_prompt_kernel, wk=wk),
        grid=(b, nq),
        in_specs=[pl.BlockSpec((D_HEADS, tq, D_KV_LORA), lambda i, j: (0, i * nq + j, 0)),
                  pl.BlockSpec((D_HEADS, tq, D_ROPE), lambda i, j: (0, i * nq + j, 0)),
                  pl.BlockSpec((1, t, D_KV_LORA), lambda i, j: (i, 0, 0)),
                  pl.BlockSpec((1, t, D_ROPE), lambda i, j: (i, 0, 0))],
        out_specs=pl.BlockSpec((D_HEADS, tq, D_KV_LORA), lambda i, j: (0, i * nq + j, 0)),
        out_shape=jax.ShapeDtypeStruct((D_HEADS, b * t, D_KV_LORA), BF16),
        scratch_shapes=[pltpu.VMEM((D_HEADS * tq, LANE), F32),
                        pltpu.VMEM((D_HEADS * tq, LANE), F32),
                        pltpu.VMEM((D_HEADS * tq, D_KV_LORA), F32)],
        compiler_params=_cparams(("parallel", "arbitrary")),
        name="mla_prompt",
    )(qa, qp, ckv, kpe)


def _page_copy(pool_ref, buf_ref, sem_ref, layer, page, slot, row0, which):
    if buf_ref.shape[-1] == pool_ref.shape[-1] and pool_ref.shape[-2] == PAGE_SIZE:
        window = (slot, pl.ds(row0, PAGE_SIZE))
    else:
        window = (slot,) + (slice(None),) * (len(pool_ref.shape) - 3) + (pl.ds(row0, PAGE_SIZE),)
    return pltpu.make_async_copy(pool_ref.at[layer, page], buf_ref.at[window], sem_ref.at[slot, which])


def _fetch_pages(pt_ref, pools, bufs, sem_ref, layer, seq, slot, n_pages, wait):
    def body(p, carry):
        page = 0 if wait else pt_ref[seq, p]
        for which, (pool_ref, buf_ref) in enumerate(zip(pools, bufs)):
            cp = _page_copy(pool_ref, buf_ref, sem_ref, layer, page, slot, p * PAGE_SIZE, which)
            if wait:
                cp.wait()
            else:
                cp.start()
        return carry

    lax.fori_loop(0, n_pages, body, 0)


def _paged_prologue(pt_ref, pools, bufs, sem_ref, layer, n_pages, seq, n_seq):
    slot = seq % 2
    args = (pt_ref, pools, bufs, sem_ref, layer)

    @pl.when(seq == 0)
    def _():
        _fetch_pages(*args, seq, slot, n_pages, wait=False)

    @pl.when(seq + 1 < n_seq)
    def _():
        _fetch_pages(*args, seq + 1, 1 - slot, n_pages, wait=False)

    _fetch_pages(*args, seq, slot, n_pages, wait=True)
    return slot


def _dsa_decode_keys_kernel(pt_ref, qi_ref, wi_ref, kin_ref, kidx_pool, key_ref, kidx_buf, sem_ref,
                            *, layer, n_pages, wk):
    past = n_pages * PAGE_SIZE
    n_seq = pl.num_programs(0) * DEC_SEQS
    lane = lax.broadcasted_iota(I32, (1, wk), 1)
    for s in range(DEC_SEQS):
        seq = pl.program_id(0) * DEC_SEQS + s
        slot = _paged_prologue(pt_ref, (kidx_pool,), (kidx_buf,), sem_ref, layer, n_pages, seq, n_seq)
        qi = qi_ref[s]
        wi = wi_ref[s]
        kidx_t = kidx_buf[slot].astype(BF16)
        e_past = _dot(qi, kidx_t) * (C_IDX_HD ** -0.5)
        key_past = _sortable_key(jnp.sum(wi * jnp.maximum(e_past, 0.0), axis=0, keepdims=True))
        kin = kin_ref[s].astype(BF16).astype(F32)
        e_new = jnp.sum(qi.astype(F32) * kin, axis=1, keepdims=True) * (C_IDX_HD ** -0.5)
        key_new = _sortable_key(jnp.sum(wi * jnp.maximum(e_new, 0.0), axis=0, keepdims=True))
        for c in range(past // wk):
            key_ref[c, s:s + 1, :] = key_past[:, c * wk:(c + 1) * wk]
        key_ref[past // wk, s:s + 1, :] = jnp.where(lane == 0, key_new, INT_MIN)


def dsa_decode_keys(page_table, qi, wi, ki_new, kidx_pool, layer):
    b, n_pages = page_table.shape
    past = n_pages * PAGE_SIZE
    wk = min(KEY_CHUNK, past)
    n_chunks = past // wk + 1
    grid_spec = pltpu.PrefetchScalarGridSpec(
        num_scalar_prefetch=1,
        grid=(b // DEC_SEQS,),
        in_specs=[pl.BlockSpec((DEC_SEQS, C_IDX_HEADS, C_IDX_HD), lambda i, pt: (i, 0, 0)),
                  pl.BlockSpec((DEC_SEQS, C_IDX_HEADS, 1), lambda i, pt: (i, 0, 0)),
                  pl.BlockSpec((DEC_SEQS, 1, C_IDX_HD), lambda i, pt: (i, 0, 0)),
                  pl.BlockSpec(memory_space=pl.ANY)],
        out_specs=pl.BlockSpec((n_chunks, DEC_SEQS, wk), lambda i, pt: (0, i, 0)),
        scratch_shapes=[pltpu.VMEM((2, C_IDX_HD, past), F32),
                        pltpu.SemaphoreType.DMA((2, 1))])
    return pl.pallas_call(
        functools.partial(_dsa_decode_keys_kernel, layer=layer, n_pages=n_pages, wk=wk),
        grid_spec=grid_spec,
        out_shape=jax.ShapeDtypeStruct((n_chunks, b, wk), I32),
        compiler_params=_cparams(("arbitrary",)),
        name="dsa_decode_keys",
    )(page_table, qi, wi, ki_new, kidx_pool)


def _dsa_select_kernel(key_ref, thr_ref, cut_ref, *, n_sel):
    n_chunks, rows, wk = key_ref.shape
    thr, cut = _select_threshold(key_ref, n_chunks, rows, wk, n_sel, n_chunks * wk)
    thr_ref[...] = thr
    cut_ref[...] = cut


def dsa_select(keys, n_sel):
    n_chunks, b, wk = keys.shape
    rows = _pick_tile(b, 128)
    return pl.pallas_call(
        functools.partial(_dsa_select_kernel, n_sel=n_sel),
        grid=(b // rows,),
        in_specs=[pl.BlockSpec((n_chunks, rows, wk), lambda i: (0, i, 0))],
        out_specs=[pl.BlockSpec((rows, 1), lambda i: (i, 0)), pl.BlockSpec((rows, 1), lambda i: (i, 0))],
        out_shape=[jax.ShapeDtypeStruct((b, 1), I32), jax.ShapeDtypeStruct((b, 1), I32)],
        compiler_params=_cparams(("parallel",)),
        name="dsa_select",
    )(keys)


def _dsa_decode_attn_kernel(pt_ref, q_ref, key_ref, thr_ref, cut_ref, kn_ref, vn_ref, k_pool, v_pool, o_ref,
                            k_buf, v_buf, sem_ref, *, layer, n_pages):
    past = n_pages * PAGE_SIZE
    slot = _paged_prologue(pt_ref, (k_pool, v_pool), (k_buf, v_buf), sem_ref, layer, n_pages,
                           pl.program_id(0), pl.num_programs(0))
    thr = thr_ref[0]
    cut = cut_ref[0]
    tie_ok = thr != INT_MIN
    key_past = key_ref[0, :, :past]
    key_new = key_ref[0, :, past:past + 1]
    pos = lax.broadcasted_iota(I32, (1, past), 1)
    sel_past = (key_past > thr) | ((key_past == thr) & (pos <= cut) & tie_ok)
    sel_new = (key_new > thr) | ((key_new == thr) & (cut >= past) & tie_ok)

    grp = C_HEADS // C_KV_HEADS
    for n in range(C_KV_HEADS):
        qn = q_ref[0, n * grp:(n + 1) * grp]
        lanes = slice(n * C_HD, (n + 1) * C_HD)
        kp_t = k_buf[slot, n].astype(BF16)
        vp_t = v_buf[slot, n].astype(BF16)
        s = jnp.where(sel_past, _dot(qn, kp_t) * (C_HD ** -0.5), NEG)
        k_new = kn_ref[0, :, lanes].astype(BF16).astype(F32)
        v_new = vn_ref[0, :, lanes].astype(BF16).astype(F32)
        s_new = jnp.sum(qn.astype(F32) * k_new, axis=1, keepdims=True) * (C_HD ** -0.5)
        s_new = jnp.where(sel_new, s_new, NEG)
        m = jnp.maximum(jnp.max(s, axis=1, keepdims=True), s_new)
        p = jnp.where(sel_past, jnp.exp(s - m), 0.0)
        p_new = jnp.where(sel_new, jnp.exp(s_new - m), 0.0)
        denom = jnp.sum(p, axis=1, keepdims=True) + p_new
        num = _dot_nt(p.astype(BF16), vp_t) + p_new.astype(BF16).astype(F32) * v_new
        o_ref[0, n * grp:(n + 1) * grp] = num / denom


def dsa_decode_attn(page_table, q, key_rows, thr, cut, k_new, v_new, k_pool, v_pool, layer):
    b, n_pages = page_table.shape
    past = n_pages * PAGE_SIZE
    n_pos = key_rows.shape[2]
    kvw = C_KV_HEADS * C_HD
    any_spec = pl.BlockSpec(memory_space=pl.ANY)
    grid_spec = pltpu.PrefetchScalarGridSpec(
        num_scalar_prefetch=1,
        grid=(b,),
        in_specs=[pl.BlockSpec((1, C_HEADS, C_HD), lambda i, pt: (i, 0, 0)),
                  pl.BlockSpec((1, 1, n_pos), lambda i, pt: (i, 0, 0)),
                  pl.BlockSpec((1, 1, 1), lambda i, pt: (i, 0, 0)),
                  pl.BlockSpec((1, 1, 1), lambda i, pt: (i, 0, 0)),
                  pl.BlockSpec((1, 1, kvw), lambda i, pt: (i, 0, 0)),
                  pl.BlockSpec((1, 1, kvw), lambda i, pt: (i, 0, 0)),
                  any_spec, any_spec],
        out_specs=pl.BlockSpec((1, C_HEADS, C_HD), lambda i, pt: (i, 0, 0)),
        scratch_shapes=[pltpu.VMEM((2, C_KV_HEADS, C_HD, past), F32),
                        pltpu.VMEM((2, C_KV_HEADS, C_HD, past), F32),
                        pltpu.SemaphoreType.DMA((2, 2))])
    return pl.pallas_call(
        functools.partial(_dsa_decode_attn_kernel, layer=layer, n_pages=n_pages),
        grid_spec=grid_spec,
        out_shape=jax.ShapeDtypeStruct((b, C_HEADS, C_HD), F32),
        compiler_params=_cparams(("arbitrary",)),
        name="dsa_decode_attn",
    )(page_table, q, key_rows, thr, cut, k_new, v_new, k_pool, v_pool)


def _mla_decode_kernel(pt_ref, qa_ref, qp_ref, cn_ref, pn_ref, ckv_pool, kpe_pool, o_ref,
                       ckv_buf, kpe_buf, sem_ref, *, layer, n_pages):
    slot = _paged_prologue(pt_ref, (ckv_pool, kpe_pool), (ckv_buf, kpe_buf), sem_ref, layer, n_pages,
                           pl.program_id(0), pl.num_programs(0))
    scale = (D_NOPE + D_ROPE) ** -0.5
    qa = qa_ref[0]
    qp = qp_ref[0]
    ckv = ckv_buf[slot].astype(BF16)
    kpe_t = kpe_buf[slot].astype(BF16)
    s = (_dot_nt(qa, ckv) + _dot(qp, kpe_t)) * scale
    c_new = cn_ref[0].astype(BF16).astype(F32)
    p_new_k = pn_ref[0].astype(BF16).astype(F32)
    s_new = (jnp.sum(qa.astype(F32) * c_new, axis=1, keepdims=True)
             + jnp.sum(qp.astype(F32) * p_new_k, axis=1, keepdims=True)) * scale
    m = jnp.maximum(jnp.max(s, axis=1, keepdims=True), s_new)
    p = jnp.exp(s - m)
    p_new = jnp.exp(s_new - m)
    denom = jnp.sum(p, axis=1, keepdims=True) + p_new
    num = _dot(p.astype(BF16), ckv) + p_new.astype(BF16).astype(F32) * c_new
    o_ref[0] = (num / denom).astype(o_ref.dtype)


def mla_decode(page_table, qa, qp, ckv_new, kpe_new, ckv_pool, kpe_pool, layer):
    b, n_pages = page_table.shape
    past = n_pages * PAGE_SIZE
    any_spec = pl.BlockSpec(memory_space=pl.ANY)
    grid_spec = pltpu.PrefetchScalarGridSpec(
        num_scalar_prefetch=1,
        grid=(b,),
        in_specs=[pl.BlockSpec((1, D_HEADS, D_KV_LORA), lambda i, pt: (i, 0, 0)),
                  pl.BlockSpec((1, D_HEADS, D_ROPE), lambda i, pt: (i, 0, 0)),
                  pl.BlockSpec((1, 1, D_KV_LORA), lambda i, pt: (i, 0, 0)),
                  pl.BlockSpec((1, 1, D_ROPE), lambda i, pt: (i, 0, 0)),
                  any_spec, any_spec],
        out_specs=pl.BlockSpec((1, D_HEADS, D_KV_LORA), lambda i, pt: (i, 0, 0)),
        scratch_shapes=[pltpu.VMEM((2, past, D_KV_LORA), F32),
                        pltpu.VMEM((2, D_ROPE, past), F32),
                        pltpu.SemaphoreType.DMA((2, 2))])
    return pl.pallas_call(
        functools.partial(_mla_decode_kernel, layer=layer, n_pages=n_pages),
        grid_spec=grid_spec,
        out_shape=jax.ShapeDtypeStruct((b, D_HEADS, D_KV_LORA), BF16),
        compiler_params=_cparams(("arbitrary",)),
        name="mla_decode",
    )(page_table, qa, qp, ckv_new, kpe_new, ckv_pool, kpe_pool)


def _rmsnorm(x, g, eps=EPS):
    return x * lax.rsqrt(jnp.mean(x * x, axis=-1, keepdims=True) + eps) * g


def _rope(x, pos):
    half = x.shape[-1] // 2
    inv = ROPE_THETA ** (-jnp.arange(half, dtype=F32) / half)
    ang = pos.astype(F32)[:, None] * inv[None, :]
    shape = (x.shape[0],) + (1,) * (x.ndim - 2) + (half,)
    cos = jnp.cos(ang).reshape(shape)
    sin = jnp.sin(ang).reshape(shape)
    x1, x2 = x[..., :half], x[..., half:]
    return jnp.concatenate([x1 * cos - x2 * sin, x2 * cos + x1 * sin], axis=-1)


def _layer(h, layer, wts, small, cache, page_table, state, dims):
    bp, tp, bs = dims
    p_rows = bp * tp
    m = h.shape[0]
    prm = {name: arr[layer] for name, arr in small.items()}
    pos = jnp.concatenate([jnp.tile(jnp.arange(tp, dtype=I32), bp),
                           jnp.full((bs,), page_table.shape[1] * PAGE_SIZE, I32)])

    xn, z = norm_matmul(h, wts['norm_mix_g'], wts['w_mix'], layer, tn=640, emit_xn=True)
    o1 = A_IN
    o2 = o1 + B_IN
    o3 = o2 + C_PAD
    za, zb, zc, zd = z[:, :o1], z[:, o1:o2], z[:, o2:o2 + C_IN], z[:, o3:o3 + D_IN]

    za_p = za[:p_rows].reshape(bp, tp, A_IN)
    prev_p = jnp.concatenate([jnp.zeros((bp, 1, A_IN), F32), za_p[:, :-1]], axis=1).reshape(p_rows, A_IN)
    z_prev = jnp.concatenate([prev_p, state['shift']], axis=0)
    zs = za + (z_prev - za) * prm['a_mu']
    r, k, v = zs[:, :BR], zs[:, BR:2 * BR], zs[:, 2 * BR:3 * BR]
    xw, xa = zs[:, 3 * BR:3 * BR + A_DECAY_LORA], zs[:, 3 * BR + A_DECAY_LORA:]
    lora = matmul(jnp.concatenate([jnp.tanh(xw), xa], axis=1).astype(BF16), wts['a_lora'], layer, tn=2 * BR)
    decay = jnp.exp(-DECAY_SCALE * jax.nn.sigmoid(prm['a_w0'] + lora[:, :BR]))
    a = jax.nn.sigmoid(prm['a_a0'] + lora[:, BR:])
    hd = lambda x: x.reshape(m, A_HEADS, A_HD)
    kk = hd(k * prm['a_kk'])
    kk = kk / jnp.maximum(jnp.sqrt(jnp.sum(kk * kk, axis=-1, keepdims=True)), 1e-12)
    k2 = hd(k * (1.0 + (a - 1.0) * prm['a_ka']))
    r3, v3, w3, a3 = hd(r), hd(v), hd(decay), hd(a)
    bonus = jnp.sum(r3 * k2 * prm['a_rk'], axis=-1, keepdims=True) * v3

    def scan_group(sl, b, t, s0):
        tr = lambda x: jnp.transpose(x[sl].reshape(b, t, A_HEADS, A_HD), (0, 2, 1, 3))
        y, s_t = rwkv_scan(tr(r3), tr(w3), tr(k2), tr(v3), tr(-kk), tr(kk * a3), s0)
        return jnp.transpose(y, (0, 2, 1, 3)).reshape(b * t, A_HEADS, A_HD), s_t

    y_p, s_p = scan_group(slice(0, p_rows), bp, tp, jnp.zeros((bp, A_HEADS, A_HD, A_HD), F32))
    y_s, s_s = scan_group(slice(p_rows, m), bs, 1, state['rwkv'])
    y = jnp.concatenate([y_p, y_s], axis=0)
    mu = jnp.mean(y, axis=-1, keepdims=True)
    var = jnp.mean(jnp.square(y - mu), axis=-1, keepdims=True)
    y = ((y - mu) * lax.rsqrt(var + A_LN_EPS) * prm['a_lnx_g'].reshape(A_HEADS, A_HD)
         + prm['a_lnx_b'].reshape(A_HEADS, A_HD))
    o_a = (y + bonus).reshape(m, BR)
    shift_p = za_p[:, -1]
    shift_s = za[p_rows:]

    ob_p = chunk_mlp(zb[:p_rows].reshape(bp, tp, B_IN), prm['b_ln_g'], prm['b_ln_b'],
                     prm['b_ws'], prm['b_bs']).reshape(p_rows, BR)
    zb_s = zb[p_rows:]
    xs = zb_s[:, BR:]
    mu = jnp.mean(xs, axis=-1, keepdims=True)
    var = jnp.mean(jnp.square(xs - mu), axis=-1, keepdims=True)
    v_rows = (xs - mu) * lax.rsqrt(var + EPS) * prm['b_ln_g'] + prm['b_ln_b']
    w00 = jnp.repeat(prm['b_ws'][:, 0, 0], B_GC)
    b00 = jnp.repeat(prm['b_bs'][:, 0], B_GC)
    ob_s = zb_s[:, :BR] * (v_rows.astype(BF16).astype(F32) * w00.astype(BF16).astype(F32) + b00)
    o_b = jnp.concatenate([ob_p, ob_s], axis=0)

    c1 = C_HEADS * C_HD
    c2 = c1 + C_KV_HEADS * C_HD
    c3 = c2 + C_KV_HEADS * C_HD
    c4 = c3 + C_IDX_HEADS * C_IDX_HD
    c5 = c4 + C_IDX_HD
    q = _rope(_rmsnorm(zc[:, :c1].reshape(m, C_HEADS, C_HD), prm['c_qn_g']), pos)
    kc = _rope(_rmsnorm(zc[:, c1:c2].reshape(m, C_KV_HEADS, C_HD), prm['c_kn_g']), pos)
    vc = zc[:, c2:c3].reshape(m, C_KV_HEADS, C_HD)
    qi = _rope(zc[:, c3:c4].reshape(m, C_IDX_HEADS, C_IDX_HD), pos)
    ki = _rope(zc[:, c4:c5], pos)
    wi = zc[:, c5:] * (C_IDX_HEADS ** -0.5)
    q_b, qi_b = q.astype(BF16), qi.astype(BF16)
    kvp = lambda x: jnp.transpose(x[:p_rows].reshape(bp, tp, C_KV_HEADS, C_HD), (0, 2, 1, 3)).astype(BF16)
    oc_p = dsa_prompt(jnp.transpose(q_b[:p_rows], (1, 0, 2)), jnp.transpose(qi_b[:p_rows], (1, 0, 2)),
                      wi[:p_rows], ki[:p_rows].reshape(bp, tp, C_IDX_HD).astype(BF16), kvp(kc), kvp(vc))
    oc_p = jnp.transpose(oc_p, (1, 0, 2)).reshape(p_rows, BR)
    kvw = C_KV_HEADS * C_HD
    past = page_table.shape[1] * PAGE_SIZE
    keys = dsa_decode_keys(page_table, qi_b[p_rows:], wi[p_rows:, :, None], ki[p_rows:, None, :],
                           cache['kidx'], layer)
    thr, cut = dsa_select(keys, min(C_TOPK, (past + 1) // 4))
    key_rows = jnp.transpose(keys, (1, 0, 2)).reshape(bs, 1, -1)
    oc_s = dsa_decode_attn(page_table, q_b[p_rows:], key_rows, thr[:, :, None], cut[:, :, None],
                           kc[p_rows:].reshape(bs, 1, kvw), vc[p_rows:].reshape(bs, 1, kvw),
                           cache['k'], cache['v'], layer)
    o_c = jnp.concatenate([oc_p, oc_s.reshape(bs, BR)], axis=0)

    d1 = D_Q_LORA
    d2 = d1 + D_KV_LORA
    cq = _rmsnorm(zd[:, :d1], prm['d_qa_norm_g'])
    qd = matmul(cq.astype(BF16), wts['d_w_uq'], layer, tn=D_HEADS * (D_NOPE + D_ROPE))
    qd = _rmsnorm(qd.reshape(m, D_HEADS, D_NOPE + D_ROPE), prm['d_qn_g'])
    q_nope = qd[..., :D_NOPE].reshape(m, D_HEADS * D_NOPE)
    q_pe = jnp.transpose(_rope(qd[..., D_NOPE:], pos), (1, 0, 2)).astype(BF16)
    ckv = _rmsnorm(zd[:, d1:d2], prm['d_kva_norm_g'])
    kpe = _rope(_rmsnorm(zd[:, d2:], prm['d_kn_g']), pos)
    q_abs = headmm_to_headmajor(q_nope.astype(BF16), wts['d_w_uk'], layer, out_dtype=BF16)
    ol_p = mla_prompt(q_abs[:, :p_rows], q_pe[:, :p_rows],
                      ckv[:p_rows].reshape(bp, tp, D_KV_LORA).astype(BF16),
                      kpe[:p_rows].reshape(bp, tp, D_ROPE).astype(BF16))
    ol_s = mla_decode(page_table, jnp.transpose(q_abs[:, p_rows:], (1, 0, 2)),
                      jnp.transpose(q_pe[:, p_rows:], (1, 0, 2)),
                      ckv[p_rows:, None, :], kpe[p_rows:, None, :], cache['ckv'], cache['kpe'], layer)
    o_lat = jnp.concatenate([ol_p, jnp.transpose(ol_s, (1, 0, 2))], axis=1)
    o_d = headmm_from_headmajor(o_lat, wts['d_w_uv'], layer, out_dtype=BF16)

    br = jnp.stack([o_a.astype(BF16), o_b.astype(BF16), o_c.astype(BF16), o_d], axis=0)
    merged = gate_merge(xn, br, wts['w_gate'], wts['w_branch'], layer)
    h = matmul_residual(merged, wts['w_out'], layer, h)

    up = norm_matmul(h, wts['norm_ffn_g'], wts['f_up'], layer, tn=512, emit_xn=False)
    act = conv_swiglu(up, jnp.transpose(state['conv'], (1, 0, 2)), wts['f_cw'], wts['f_cb'], layer,
                      t_prompt=tp, n_sample=bs)
    conv_p = jnp.stack([up[(b + 1) * tp - 2:(b + 1) * tp] for b in range(bp)], axis=0)
    conv_s = jnp.stack([state['conv'][:, 1], up[p_rows:]], axis=1)
    h = matmul_residual(act, wts['f_down'], layer, h)

    h = ple_update(h, wts['norm_ple_g'], wts['w_ple_gate'], wts['pe'], wts['w_ple_proj'], layer)

    st_p = (s_p, shift_p, conv_p, kc[:p_rows].reshape(bp, tp, C_KV_HEADS, C_HD),
            vc[:p_rows].reshape(bp, tp, C_KV_HEADS, C_HD), ki[:p_rows].reshape(bp, tp, C_IDX_HD),
            ckv[:p_rows].reshape(bp, tp, D_KV_LORA), kpe[:p_rows].reshape(bp, tp, D_ROPE))
    st_s = (s_s, shift_s, conv_s, kc[p_rows:].reshape(bs, 1, C_KV_HEADS, C_HD),
            vc[p_rows:].reshape(bs, 1, C_KV_HEADS, C_HD), ki[p_rows:].reshape(bs, 1, C_IDX_HD),
            ckv[p_rows:].reshape(bs, 1, D_KV_LORA), kpe[p_rows:].reshape(bs, 1, D_ROPE),
            v_rows.reshape(bs, 1, BR))
    return h, st_p, st_s


def kernel(x_prompt, x_sample, state_rwkv, state_rwkv_shift, state_ffn_conv, cache_dsa_k, cache_dsa_v, cache_dsa_kidx, cache_mla_ckv, cache_mla_kpe, page_table, p_prompt, p_sample, norm_mix_g, w_in, a_mu, a_w0, a_w2, a_a0, a_a2, a_kk, a_ka, a_rk, a_lnx_g, a_lnx_b, b_ln_g, b_ln_b, b_ws, b_bs, c_qn_g, c_kn_g, d_qa_norm_g, d_w_uq, d_qn_g, d_kva_norm_g, d_kn_g, d_w_uk, d_w_uv, w_branch, w_out, norm_ffn_g, f_up, f_cw, f_cb, f_down, norm_ple_g, w_ple_gate, w_ple_proj):
    bp, tp, _ = x_prompt.shape
    bs = x_sample.shape[0]
    depth = w_in.shape[0]
    p_rows = bp * tp
    h = jnp.concatenate([x_prompt.reshape(p_rows, D_MODEL), x_sample.reshape(bs, D_MODEL)], axis=0)
    o2 = A_IN + B_IN
    o3 = o2 + C_IN
    o4 = o3 + D_IN
    w_in_b = w_in.astype(BF16)
    pad = lambda x, n: jnp.pad(x, ((0, 0), (0, 0), (0, n - x.shape[2])))
    zeros = jnp.zeros((depth, A_DECAY_LORA, BR), F32)
    gain = lambda g: g.reshape(depth, 1, D_MODEL)
    wts = dict(
        w_mix=jnp.concatenate([w_in_b[:, :, :o2], pad(w_in_b[:, :, o2:o3], C_PAD), pad(w_in_b[:, :, o3:o4], D_PAD)], axis=2),
        w_gate=w_in_b[:, :, o4:],
        a_lora=jnp.concatenate([jnp.concatenate([a_w2, zeros], axis=2),
                                jnp.concatenate([zeros, a_a2], axis=2)], axis=1).astype(BF16),
        d_w_uq=d_w_uq.astype(BF16),
        d_w_uk=jnp.transpose(d_w_uk, (0, 2, 3, 1)).astype(BF16),
        d_w_uv=jnp.transpose(d_w_uv, (0, 2, 1, 3)).astype(BF16),
        w_branch=w_branch.astype(BF16), w_out=w_out.astype(BF16),
        f_up=f_up.astype(BF16), f_down=f_down.astype(BF16),
        w_ple_gate=w_ple_gate.astype(BF16), w_ple_proj=w_ple_proj.astype(BF16),
        norm_mix_g=gain(norm_mix_g), norm_ffn_g=gain(norm_ffn_g), norm_ple_g=gain(norm_ple_g),
        pe=jnp.concatenate([p_prompt.reshape(depth, p_rows, -1), p_sample.reshape(depth, bs, -1)], axis=1).astype(BF16))
    small = dict(a_mu=a_mu, a_w0=a_w0, a_a0=a_a0, a_kk=a_kk, a_ka=a_ka, a_rk=a_rk, a_lnx_g=a_lnx_g,
                 a_lnx_b=a_lnx_b, b_ln_g=b_ln_g, b_ln_b=b_ln_b, b_ws=b_ws, b_bs=b_bs, c_qn_g=c_qn_g,
                 c_kn_g=c_kn_g, d_qa_norm_g=d_qa_norm_g, d_qn_g=d_qn_g, d_kva_norm_g=d_kva_norm_g,
                 d_kn_g=d_kn_g)
    wts.update(f_cw=f_cw, f_cb=f_cb.reshape(depth, 1, -1))
    kvw = C_KV_HEADS * C_HD
    cache = dict(k=jnp.transpose(cache_dsa_k, (0, 1, 3, 4, 2)),
                 v=jnp.transpose(cache_dsa_v, (0, 1, 3, 4, 2)),
                 kidx=jnp.swapaxes(cache_dsa_kidx, 2, 3), ckv=cache_mla_ckv,
                 kpe=jnp.swapaxes(cache_mla_kpe, 2, 3))
    sp, ss = [], []
    for i in range(depth):
        state = dict(rwkv=state_rwkv[i], shift=state_rwkv_shift[i], conv=state_ffn_conv[i])
        h, st_p, st_s = _layer(h, i, wts, small, cache, page_table, state, (bp, tp, bs))
        sp.append(st_p)
        ss.append(st_s)

    def stk(lst, j):
        return jnp.stack([s[j] for s in lst], axis=0)

    return ((h[:p_rows].reshape(bp, tp, D_MODEL), h[p_rows:].reshape(bs, 1, D_MODEL))
            + tuple(stk(sp, j) for j in range(8)) + tuple(stk(ss, j) for j in range(9)))
```
